```python
import jax, jax.numpy as jnp
from jax import lax
import numpy as np

D_MODEL = 1024
BATCH = 16
SEQ = 4096
DEPTH = 2

CHUNK = 64
HEAD_DIM = 64
ROT_DIM = HEAD_DIM // 4
ROPE_THETA = 500000.0
RMS_EPS = 1e-6
LN_EPS = 1e-6
NEG_INF = -1e30

FOX_HEADS = 8
FOX_WIDTH = FOX_HEADS * HEAD_DIM
FOX_BLOCK = 128
DSA_HEADS = 8
DSA_WIDTH = DSA_HEADS * HEAD_DIM
IDX_HEADS = 8
IDX_DIM = 64
DSA_TOPK = 256
DSA_BLOCK = 64
FOX_SPLITS = (FOX_WIDTH, FOX_WIDTH, FOX_WIDTH, FOX_HEADS, FOX_WIDTH)
DSA_SPLITS = (DSA_WIDTH, DSA_WIDTH, DSA_WIDTH, IDX_HEADS * IDX_DIM, IDX_DIM, IDX_HEADS, DSA_WIDTH)
IN0_DIM = sum(FOX_SPLITS) + sum(DSA_SPLITS)
MIX0_WIDTH = FOX_WIDTH + DSA_WIDTH

CHK_HEADS = 16
CHK_WIDTH = CHK_HEADS * HEAD_DIM
CHK_LEFT_CHUNKS = 8
MAX_REL_DIST = 128
CHK_SPLITS = (CHK_WIDTH, CHK_WIDTH, CHK_WIDTH, CHK_WIDTH)
IN1_DIM = sum(CHK_SPLITS)

kernel_name = "hybrid_fox_dsa_chunkattn_sandwich"


def rms_norm(x, g):
    xf = x.astype(jnp.float32)
    y = xf * lax.rsqrt(jnp.mean(xf * xf, axis=-1, keepdims=True) + RMS_EPS)
    return (y * g.astype(jnp.float32)).astype(x.dtype)


def layer_norm(x, g, b):
    xf = x.astype(jnp.float32)
    mu = jnp.mean(xf, axis=-1, keepdims=True)
    xc = xf - mu
    y = xc * lax.rsqrt(jnp.mean(xc * xc, axis=-1, keepdims=True) + LN_EPS)
    return (y * g.astype(jnp.float32) + b.astype(jnp.float32)).astype(x.dtype)


def take_cols(h, sizes):
    out, off = [], 0
    for n in sizes:
        out.append(h[..., off:off + n])
        off += n
    return out


def to_heads(t, n_heads):
    return t.reshape(t.shape[0], t.shape[1], n_heads, HEAD_DIM)


def rope_tables(positions):
    inv_freq = ROPE_THETA ** (-jnp.arange(0, ROT_DIM, 2, dtype=jnp.float32) / ROT_DIM)
    ang = positions.astype(jnp.float32)[..., None] * inv_freq
    return jnp.cos(ang), jnp.sin(ang)


def apply_partial_rope(t, cos, sin):
    half = ROT_DIM // 2
    c = cos[:, :, None, :].astype(t.dtype)
    s = sin[:, :, None, :].astype(t.dtype)
    t1, t2, tp = t[..., :half], t[..., half:ROT_DIM], t[..., ROT_DIM:]
    return jnp.concatenate([t1 * c - t2 * s, t2 * c + t1 * s, tp], axis=-1)


def forgetting_attention(q, k, v, f_logit):
    B, S, H, D = q.shape
    F = jnp.cumsum(jax.nn.log_sigmoid(f_logit.astype(jnp.float32)), axis=1)
    Fh = F.transpose(0, 2, 1)
    scale = D ** -0.5
    key_pos = jnp.arange(S)

    def block(i):
        start = i * FOX_BLOCK
        qb = lax.dynamic_slice_in_dim(q, start, FOX_BLOCK, axis=1)
        Fq = lax.dynamic_slice_in_dim(Fh, start, FOX_BLOCK, axis=2)
        s = jnp.einsum('bqhd,bkhd->bhqk', qb, k, preferred_element_type=jnp.float32) * scale
        s = s + Fq[..., None] - Fh[:, :, None, :]
        q_pos = start + jnp.arange(FOX_BLOCK)
        causal = key_pos[None, :] <= q_pos[:, None]
        s = jnp.where(causal[None, None], s, NEG_INF)
        p = jax.nn.softmax(s, axis=-1)
        return jnp.einsum('bhqk,bkhd->bqhd', p.astype(v.dtype), v)

    out = lax.map(block, jnp.arange(S // FOX_BLOCK))
    return out.transpose(1, 0, 2, 3, 4).reshape(B, S, H * D)


def dsa_attention(q, k, v, iq, ik, iw, top_k):
    B, S, H, D = q.shape
    scale = D ** -0.5
    idx_scale = (IDX_HEADS * IDX_DIM) ** -0.5
    key_pos = jnp.arange(S)

    def block(i):
        start = i * DSA_BLOCK
        qb = lax.dynamic_slice_in_dim(q, start, DSA_BLOCK, axis=1)
        iqb = lax.dynamic_slice_in_dim(iq, start, DSA_BLOCK, axis=1)
        iwb = lax.dynamic_slice_in_dim(iw, start, DSA_BLOCK, axis=1)
        q_pos = start + jnp.arange(DSA_BLOCK)
        limit = (q_pos // CHUNK + 1) * CHUNK
        admissible = key_pos[None, :] < limit[:, None]
        rel = jax.nn.relu(jnp.einsum('bqhd,bkd->bqhk', iqb, ik, preferred_element_type=jnp.float32))
        score = jnp.einsum('bqhk,bqh->bqk', rel, iwb.astype(jnp.float32)) * idx_scale
        score = jnp.where(admissible[None], score, NEG_INF)
        _, sel = lax.top_k(score, top_k)
        ksel = jax.vmap(lambda kb, ib: kb[ib])(k, sel)
        vsel = jax.vmap(lambda vb, ib: vb[ib])(v, sel)
        s = jnp.einsum('bqhd,bqkhd->bhqk', qb, ksel, preferred_element_type=jnp.float32) * scale
        ok = sel < limit[None, :, None]
        s = jnp.where(ok[:, None], s, NEG_INF)
        p = jax.nn.softmax(s, axis=-1)
        return jnp.einsum('bhqk,bqkhd->bqhd', p.astype(v.dtype), vsel)

    out = lax.map(block, jnp.arange(S // DSA_BLOCK))
    return out.transpose(1, 0, 2, 3, 4).reshape(B, S, H * D)


def chunk_band_attention(q, k, v, rel_table):
    B, S, H, D = q.shape
    scale = D ** -0.5
    left = CHK_LEFT_CHUNKS * CHUNK
    band = left + CHUNK
    kp = jnp.pad(k, ((0, 0), (left, 0), (0, 0), (0, 0)))
    vp = jnp.pad(v, ((0, 0), (left, 0), (0, 0), (0, 0)))
    i = jnp.arange(CHUNK)
    j = jnp.arange(band)
    dist = i[:, None] - j[None, :] + left
    bucket = jnp.clip(dist, -MAX_REL_DIST, MAX_REL_DIST) + MAX_REL_DIST
    bias = rel_table[:, bucket].astype(jnp.float32)

    def block(c):
        start = c * CHUNK
        qb = lax.dynamic_slice_in_dim(q, start, CHUNK, axis=1)
        kb = lax.dynamic_slice_in_dim(kp, start, band, axis=1)
        vb = lax.dynamic_slice_in_dim(vp, start, band, axis=1)
        s = jnp.einsum('bqhd,bkhd->bhqk', qb, kb, preferred_element_type=jnp.float32) * scale
        s = s + bias[None]
        valid = j >= left - start
        s = jnp.where(valid[None, None, None, :], s, NEG_INF)
        p = jax.nn.softmax(s, axis=-1)
        return jnp.einsum('bhqk,bkhd->bqhd', p.astype(vb.dtype), vb)

    out = lax.map(block, jnp.arange(S // CHUNK))
    return out.transpose(1, 0, 2, 3, 4).reshape(B, S, H * D)


def even_layer(x, cos, sin, pre_g, post_g, w_in, b_forget, idx_k_g, idx_k_b, w_out, top_k):
    B, S, _ = x.shape
    h = rms_norm(x, pre_g)
    proj = jnp.einsum('bsd,de->bse', h, w_in)
    fq, fk, fv, f_logit, f_gate, dq, dk, dv, iq, ik, iw, d_gate = take_cols(proj, FOX_SPLITS + DSA_SPLITS)
    fox = forgetting_attention(to_heads(fq, FOX_HEADS), to_heads(fk, FOX_HEADS),
                               to_heads(fv, FOX_HEADS), f_logit + b_forget)
    fox = fox * jax.nn.silu(f_gate)
    dq = apply_partial_rope(to_heads(dq, DSA_HEADS), cos, sin)
    dk = apply_partial_rope(to_heads(dk, DSA_HEADS), cos, sin)
    dv = to_heads(dv, DSA_HEADS)
    iq = apply_partial_rope(iq.reshape(B, S, IDX_HEADS, IDX_DIM), cos, sin)
    ik = apply_partial_rope(layer_norm(ik, idx_k_g, idx_k_b)[:, :, None, :], cos, sin)[:, :, 0, :]
    dsa = dsa_attention(dq, dk, dv, iq, ik, iw, top_k) * jax.nn.silu(d_gate)
    y = jnp.einsum('bse,ed->bsd', jnp.concatenate([fox, dsa], axis=-1), w_out)
    return x + rms_norm(y, post_g)


def odd_layer(x, pre_g, post_g, w_in, rel_table, w_out):
    h = rms_norm(x, pre_g)
    proj = jnp.einsum('bsd,de->bse', h, w_in)
    cq, ck, cv, c_gate = take_cols(proj, CHK_SPLITS)
    att = chunk_band_attention(to_heads(cq, CHK_HEADS), to_heads(ck, CHK_HEADS),
                               to_heads(cv, CHK_HEADS), rel_table)
    y = jnp.einsum('bse,ed->bsd', att * jax.nn.silu(c_gate), w_out)
    return x + rms_norm(y, post_g)


def setup_inputs(seed: int = 0) -> dict:
    key = jax.random.key(seed)
    ks = jax.random.split(key, 12)
    n_even = (DEPTH + 1) // 2
    n_odd = DEPTH // 2
    f32 = jnp.float32
    x = jax.random.normal(ks[0], (BATCH, SEQ, D_MODEL), f32)
    positions = jnp.broadcast_to(jnp.arange(SEQ, dtype=jnp.int32), (BATCH, SEQ))
    pre_norm_g = 1.0 + 0.05 * jax.random.normal(ks[1], (DEPTH, D_MODEL), f32)
    post_norm_g = 1.0 + 0.05 * jax.random.normal(ks[2], (DEPTH, D_MODEL), f32)
    w_in_even = jax.random.normal(ks[3], (n_even, D_MODEL, IN0_DIM), f32) * D_MODEL ** -0.5
    b_forget = jax.random.uniform(ks[4], (n_even, FOX_HEADS), f32, minval=1.0, maxval=5.0)
    idx_k_g = 1.0 + 0.05 * jax.random.normal(ks[5], (n_even, IDX_DIM), f32)
    idx_k_b = 0.02 * jax.random.normal(ks[6], (n_even, IDX_DIM), f32)
    w_out_even = jax.random.normal(ks[7], (n_even, MIX0_WIDTH, D_MODEL), f32) * MIX0_WIDTH ** -0.5
    w_in_odd = jax.random.normal(ks[8], (n_odd, D_MODEL, IN1_DIM), f32) * D_MODEL ** -0.5
    rel_bias = 0.5 * jax.random.normal(ks[9], (n_odd, CHK_HEADS, 2 * MAX_REL_DIST + 1), f32)
    w_out_odd = jax.random.normal(ks[10], (n_odd, CHK_WIDTH, D_MODEL), f32) * CHK_WIDTH ** -0.5
    return {"x": x, "positions": positions, "pre_norm_g": pre_norm_g, "post_norm_g": post_norm_g,
            "w_in_even": w_in_even, "b_forget": b_forget, "idx_k_g": idx_k_g, "idx_k_b": idx_k_b,
            "w_out_even": w_out_even, "w_in_odd": w_in_odd, "rel_bias": rel_bias, "w_out_odd": w_out_odd}


def reference(x, positions, pre_norm_g, post_norm_g, w_in_even, b_forget, idx_k_g, idx_k_b,
              w_out_even, w_in_odd, rel_bias, w_out_odd):
    seq = x.shape[1]
    top_k = min(DSA_TOPK, seq // 4)
    cos, sin = rope_tables(positions)
    for layer in range(DEPTH):
        if layer % 2 == 0:
            e = layer // 2
            x = even_layer(x, cos, sin, pre_norm_g[layer], post_norm_g[layer], w_in_even[e],
                           b_forget[e], idx_k_g[e], idx_k_b[e], w_out_even[e], top_k)
        else:
            o = layer // 2
            x = odd_layer(x, pre_norm_g[layer], post_norm_g[layer], w_in_odd[o], rel_bias[o], w_out_odd[o])
    return x
```

```python
import functools

import numpy as np
import jax
import jax.numpy as jnp
from jax import lax
from jax.experimental import pallas as pl
from jax.experimental.pallas import tpu as pltpu

F32 = jnp.float32
BF16 = jnp.bfloat16
I32 = jnp.int32

LANES = 128
HEAD_DIM = 64
HALF = HEAD_DIM
ROT_DIM = HEAD_DIM // 4
ROPE_THETA = 500000.0
RMS_EPS = 1e-6
LN_EPS = 1e-6
NEG_INF = -1e30
CHUNK = 64
CHUNK_SHIFT = 6
FOX_HEADS = 8
DSA_HEADS = 8
IDX_HEADS = 8
IDX_DIM = 64
DSA_TOPK = 256
CHK_HEADS = 16
CHK_LEFT_CHUNKS = 8
MAX_REL_DIST = 128

WIDTH = 512
T_DQ, T_DK, T_IQ, T_FQ, T_FK, T_FV, T_FG, T_DV, T_DG = range(9)
N_ROPE_TILES = 3
L_IK = 0
L_FL = 64
L_IW = 72
L_AUG = 64

VMEM_LIMIT = 52 * 1024 * 1024

INT_MIN = -2147483648


def _sortable_key_of(value):
    b = int(np.array(value, np.float32).view(np.int32))
    return b ^ ((b >> 31) & 0x7FFFFFFF)


KEY_OF_NEG_INF = _sortable_key_of(NEG_INF)


def _cparams(n_axes):
    return pltpu.CompilerParams(dimension_semantics=("arbitrary",) * n_axes,
                                vmem_limit_bytes=VMEM_LIMIT)


def _dot_t(a, b):
    return lax.dot_general(a, b, (((1,), (1,)), ((), ())), preferred_element_type=F32)


def _rope(t, cos_t, sin_t, lane):
    first = (lane & (HALF - 1)) < (ROT_DIM // 2)
    partner = jnp.where(first, pltpu.roll(t, LANES - ROT_DIM // 2, 1), pltpu.roll(t, ROT_DIM // 2, 1))
    return t * cos_t + partner * sin_t


def _split3(x):
    hi = x.astype(BF16)
    r1 = x - hi.astype(F32)
    mid = r1.astype(BF16)
    lo = (r1 - mid.astype(F32)).astype(BF16)
    return hi, mid, lo


def _proj0_kernel(x_ref, g_ref, w_ref, ws_ref, cos_ref, sin_ref, par_ref,
                  p_ref, ik_ref, sm_ref, h_scr, carry_scr, *, tm):
    s = pl.program_id(1)
    j = pl.program_id(2)
    lane = lax.broadcasted_iota(I32, (tm, LANES), 1)

    @pl.when(j == 0)
    def _prologue():
        xf = x_ref[0]
        ms = jnp.mean(xf * xf, axis=-1, keepdims=True)
        hb = ((xf * lax.rsqrt(ms + RMS_EPS)) * g_ref[...]).astype(BF16)
        h_scr[...] = hb
        small = jnp.dot(hb, ws_ref[...], preferred_element_type=F32)

        is_ik = lane < IDX_DIM
        mu = jnp.sum(jnp.where(is_ik, small, 0.0), axis=-1, keepdims=True) * (1.0 / IDX_DIM)
        xc = small - mu
        var = jnp.sum(jnp.where(is_ik, xc * xc, 0.0), axis=-1, keepdims=True) * (1.0 / IDX_DIM)
        y = xc * lax.rsqrt(var + LN_EPS) * par_ref[1:2, :] + par_ref[2:3, :]
        yr = _rope(y, cos_ref[0], sin_ref[0], lane)
        ik_ref[0] = jnp.where(is_ik, yr, pltpu.roll(yr, HALF, 1)).astype(BF16)

        z = small + par_ref[0:1, :]
        ls = jnp.minimum(z, 0.0) - jnp.log1p(jnp.exp(-jnp.abs(z)))
        is_f = (lane >= L_FL) & (lane < L_FL + FOX_HEADS)
        ls = jnp.where(is_f, ls, 0.0)
        is_iw = (lane >= L_IW) & (lane < L_IW + IDX_HEADS)
        other = small * jnp.where(is_iw, float((IDX_HEADS * IDX_DIM) ** -0.5), 1.0)

        @pl.when(s == 0)
        def _():
            carry_scr[...] = jnp.zeros_like(carry_scr)

        blk = LANES
        tri = (lax.broadcasted_iota(I32, (blk, blk), 0) >= lax.broadcasted_iota(I32, (blk, blk), 1))
        tri = jnp.where(tri, 1.0, 0.0).astype(BF16)
        carry = carry_scr[0:1, :]
        lane_blk = lax.broadcasted_iota(I32, (blk, LANES), 1)
        is_f_blk = (lane_blk >= L_FL) & (lane_blk < L_FL + FOX_HEADS)
        for r in range(tm // blk):
            hi, mid, lo = _split3(ls[r * blk:(r + 1) * blk])
            cum = (jnp.dot(tri, hi, preferred_element_type=F32)
                   + jnp.dot(tri, mid, preferred_element_type=F32)
                   + jnp.dot(tri, lo, preferred_element_type=F32)) + carry
            sm_ref[0, r * blk:(r + 1) * blk, :] = jnp.where(is_f_blk, cum, other[r * blk:(r + 1) * blk])
            carry = cum[blk - 1:blk, :]
        carry_scr[0:1, :] = carry

    res = jnp.dot(h_scr[...], w_ref[...], preferred_element_type=F32)

    @pl.when(j < N_ROPE_TILES)
    def _with_rope():
        c = cos_ref[0]
        sn = sin_ref[0]
        for hp in range(WIDTH // LANES):
            p_ref[0, 0, hp] = _rope(res[:, hp * LANES:(hp + 1) * LANES], c, sn, lane).astype(BF16)

    @pl.when(j >= N_ROPE_TILES)
    def _plain():
        for hp in range(WIDTH // LANES):
            p_ref[0, 0, hp] = res[:, hp * LANES:(hp + 1) * LANES].astype(BF16)


def _proj0(x, g, w_main, w_small, cos_t, sin_t, par, *, tm):
    B, S, D = x.shape
    n_tiles = w_main.shape[1] // WIDTH
    hp = WIDTH // LANES
    return pl.pallas_call(
        functools.partial(_proj0_kernel, tm=tm),
        grid=(B, S // tm, n_tiles),
        in_specs=[
            pl.BlockSpec((1, tm, D), lambda b, s, j: (b, s, 0)),
            pl.BlockSpec((1, D), lambda b, s, j: (0, 0)),
            pl.BlockSpec((D, WIDTH), lambda b, s, j: (0, j)),
            pl.BlockSpec((D, LANES), lambda b, s, j: (0, 0)),
            pl.BlockSpec((1, tm, LANES), lambda b, s, j: (b, s, 0)),
            pl.BlockSpec((1, tm, LANES), lambda b, s, j: (b, s, 0)),
            pl.BlockSpec((8, LANES), lambda b, s, j: (0, 0)),
        ],
        out_specs=[
            pl.BlockSpec((1, 1, hp, tm, LANES), lambda b, s, j: (j, b, 0, s, 0)),
            pl.BlockSpec((1, tm, LANES), lambda b, s, j: (b, s, 0)),
            pl.BlockSpec((1, tm, LANES), lambda b, s, j: (b, s, 0)),
        ],
        out_shape=[
            jax.ShapeDtypeStruct((n_tiles, B, hp, S, LANES), BF16),
            jax.ShapeDtypeStruct((B, S, LANES), BF16),
            jax.ShapeDtypeStruct((B, S, LANES), F32),
        ],
        scratch_shapes=[pltpu.VMEM((tm, D), BF16), pltpu.VMEM((8, LANES), F32)],
        compiler_params=_cparams(3),
        name="proj0",
    )(x, g, w_main, w_small, cos_t, sin_t, par)


def _proj1_kernel(x_ref, g_ref, w_ref, p_ref, h_scr):
    j = pl.program_id(2)

    @pl.when(j == 0)
    def _prologue():
        xf = x_ref[0]
        ms = jnp.mean(xf * xf, axis=-1, keepdims=True)
        h_scr[...] = ((xf * lax.rsqrt(ms + RMS_EPS)) * g_ref[...]).astype(BF16)

    res = jnp.dot(h_scr[...], w_ref[...], preferred_element_type=F32)
    for hp in range(WIDTH // LANES):
        p_ref[0, 0, hp] = res[:, hp * LANES:(hp + 1) * LANES].astype(BF16)


def _proj1(x, g, w, *, tm, n_tensors):
    B, S, D = x.shape
    n_tiles = w.shape[1] // WIDTH
    per = n_tiles // n_tensors
    hp = WIDTH // LANES
    return pl.pallas_call(
        _proj1_kernel,
        grid=(B, S // tm, n_tiles),
        in_specs=[
            pl.BlockSpec((1, tm, D), lambda b, s, j: (b, s, 0)),
            pl.BlockSpec((1, D), lambda b, s, j: (0, 0)),
            pl.BlockSpec((D, WIDTH), lambda b, s, j: (0, j)),
        ],
        out_specs=pl.BlockSpec((1, 1, hp, tm, LANES), lambda b, s, j: (j // per, b, j % per, s, 0)),
        out_shape=jax.ShapeDtypeStruct((n_tensors, B, per * hp, S, LANES), BF16),
        scratch_shapes=[pltpu.VMEM((tm, D), BF16)],
        compiler_params=_cparams(3),
        name="proj1",
    )(x, g, w)


def _flash_update(s_t, vt, m, l, acc):
    m_new = jnp.maximum(m, jnp.max(s_t, axis=0, keepdims=True))
    alpha = jnp.exp(m - m_new)
    p = jnp.exp(s_t - m_new)
    l_new = alpha * l + jnp.sum(p, axis=0, keepdims=True)
    acc_new = alpha * acc + jnp.dot(vt, p.astype(BF16), preferred_element_type=F32)
    return m_new, l_new, acc_new


def _flash_init(tq):
    return (jnp.full((1, tq), NEG_INF, F32), jnp.zeros((1, tq), F32), jnp.zeros((HEAD_DIM, tq), F32))


def _finish_pair(c0, c1, gate):
    acc_t = jnp.concatenate([c0[2] / c0[1], c1[2] / c1[1]], axis=0)
    g = gate.astype(F32)
    return (acc_t.T * (g * jax.nn.sigmoid(g))).astype(BF16)


def _split_pair(q, lane):
    qf = q.astype(F32)
    lo = lane < HALF
    return jnp.where(lo, qf, 0.0).astype(BF16), jnp.where(lo, 0.0, qf).astype(BF16)


def _transpose_values(v_ref_slice_fn, vt_scr_store_fn, n_blocks):
    def body(c, _):
        vt_scr_store_fn(c, v_ref_slice_fn(c).astype(F32).T.astype(BF16))
        return 0
    lax.fori_loop(0, n_blocks, body, 0)


def _fox_aug(base, f_col, lane, is_key):
    hi, mid, lo = (t.astype(F32) for t in _split3(f_col))
    if is_key:
        a, b, c, d = -hi, -mid, -lo, 1.0
        tail = jnp.where(lane < L_AUG + 6, d, 0.0)
        aug = jnp.where(lane == L_AUG, a, jnp.where(lane == L_AUG + 1, b, jnp.where(lane == L_AUG + 2, c, tail)))
    else:
        tail = jnp.where(lane == L_AUG + 3, hi, jnp.where(lane == L_AUG + 4, mid,
                                                         jnp.where(lane == L_AUG + 5, lo, 0.0)))
        aug = jnp.where(lane < L_AUG + 3, 1.0, tail)
    return jnp.where(lane < L_AUG, base, aug).astype(BF16)


def _fox_kernel(q_ref, k_ref, v_ref, g_ref, f_ref, o_ref, ka_scr, vt_scr, *, tq):
    hp = pl.program_id(1)
    qi = pl.program_id(2)
    tk = tq
    n_blocks = k_ref.shape[3] // tk
    lane = lax.broadcasted_iota(I32, (tq, LANES), 1)

    def head_f(f_tile, hh):
        sel = lane == (L_FL + 2 * hp + hh)
        return jnp.sum(jnp.where(sel, f_tile, 0.0), axis=-1, keepdims=True)

    def head_base(t, hh):
        return t if hh == 0 else pltpu.roll(t, HALF, 1)

    @pl.when(qi == 0)
    def _prepare():
        def body(c, _):
            r0 = pl.multiple_of(c * tk, tk)
            kf = k_ref[0, 0, 0, pl.ds(r0, tk), :].astype(F32)
            ff = f_ref[0, pl.ds(r0, tk), :]
            for hh in range(2):
                ka_scr[hh, pl.ds(r0, tk), :] = _fox_aug(head_base(kf, hh), head_f(ff, hh), lane, True)
            vt_scr[c] = v_ref[0, 0, 0, pl.ds(r0, tk), :].astype(F32).T.astype(BF16)
            return 0
        lax.fori_loop(0, n_blocks, body, 0)

    q0 = pl.multiple_of(qi * tq, tq)
    qf = q_ref[0, 0, 0].astype(F32)
    fq = f_ref[0, pl.ds(q0, tq), :]
    qa = [_fox_aug(head_base(qf, hh), head_f(fq, hh), lane, False) for hh in range(2)]

    def block(kb, carry, masked):
        r0 = pl.multiple_of(kb * tk, tk)
        out = []
        for hh in range(2):
            s_t = _dot_t(ka_scr[hh, pl.ds(r0, tk), :], qa[hh])
            if masked:
                future = lax.broadcasted_iota(I32, (tk, tq), 0) > lax.broadcasted_iota(I32, (tk, tq), 1)
                s_t = jnp.where(future, NEG_INF, s_t)
            vt = vt_scr[kb, hh * HEAD_DIM:(hh + 1) * HEAD_DIM, :]
            out.append(_flash_update(s_t, vt, *carry[hh]))
        return tuple(out)

    carry = lax.fori_loop(0, qi, lambda kb, c: block(kb, c, False), (_flash_init(tq), _flash_init(tq)))
    carry = block(qi, carry, True)
    o_ref[0, 0] = _finish_pair(carry[0], carry[1], g_ref[0, 0, 0])


def _fox(p0, small, *, tq):
    _, B, HP, S, _ = p0.shape
    spec_q = lambda t: pl.BlockSpec((1, 1, 1, tq, LANES), lambda b, h, q, t=t: (t, b, h, q, 0))
    spec_s = lambda t: pl.BlockSpec((1, 1, 1, S, LANES), lambda b, h, q, t=t: (t, b, h, 0, 0))
    return pl.pallas_call(
        functools.partial(_fox_kernel, tq=tq),
        grid=(B, HP, S // tq),
        in_specs=[spec_q(T_FQ), spec_s(T_FK), spec_s(T_FV), spec_q(T_FG),
                  pl.BlockSpec((1, S, LANES), lambda b, h, q: (b, 0, 0))],
        out_specs=pl.BlockSpec((1, 1, tq, LANES), lambda b, h, q: (b, h, q, 0)),
        out_shape=jax.ShapeDtypeStruct((B, HP, S, LANES), BF16),
        scratch_shapes=[pltpu.VMEM((2, S, LANES), BF16), pltpu.VMEM((S // tq, LANES, tq), BF16)],
        compiler_params=_cparams(3),
        name="fox_attn",
    )(p0, p0, p0, p0, small)


def _dsa_kernel(q_ref, k_ref, v_ref, iq_ref, g_ref, ik_ref, sm_ref, o_ref, vt_scr, key_scr,
                *, tq, top_k):
    qi = pl.program_id(1)
    tk = tq
    n_pairs = k_ref.shape[2]
    seq = k_ref.shape[3]
    n_blocks = seq // tk
    lane = lax.broadcasted_iota(I32, (tq, LANES), 1)

    @pl.when(qi == 0)
    def _prepare():
        def body(c, _):
            r0 = pl.multiple_of(c * tk, tk)
            for hp in range(n_pairs):
                vt_scr[hp, c] = v_ref[0, 0, hp, pl.ds(r0, tk), :].astype(F32).T.astype(BF16)
            return 0
        lax.fori_loop(0, n_blocks, body, 0)

    iw_t = sm_ref[0].T[L_IW:L_IW + IDX_HEADS, :]
    iq_m = []
    for hp in range(IDX_HEADS // 2):
        iq_m.extend(_split_pair(iq_ref[0, 0, hp], lane))

    def index_keys(r0):
        ikb = ik_ref[0, pl.ds(r0, tk), :]
        score = jnp.zeros((tk, tq), F32)
        for h in range(IDX_HEADS):
            score = score + jnp.maximum(_dot_t(ikb, iq_m[h]), 0.0) * iw_t[h:h + 1, :]
        bits = pltpu.bitcast(score, I32)
        return bits ^ ((bits >> 31) & 0x7FFFFFFF)

    def fill(kb, _):
        r0 = pl.multiple_of(kb * tk, tk)
        key_scr[pl.ds(r0, tk), :] = index_keys(r0)
        return 0
    lax.fori_loop(0, qi, fill, 0)

    d0 = pl.multiple_of(qi * tk, tk)
    row = lax.broadcasted_iota(I32, (tk, tq), 0)
    col = lax.broadcasted_iota(I32, (tk, tq), 1)
    beyond_chunk = (row >> CHUNK_SHIFT) > (col >> CHUNK_SHIFT)
    key_scr[pl.ds(d0, tk), :] = jnp.where(beyond_chunk, INT_MIN, index_keys(d0))

    q_pos = qi * tq + lax.broadcasted_iota(I32, (1, tq), 1)
    n_beyond = (seq - ((q_pos >> CHUNK_SHIFT) + 1) * CHUNK).astype(F32)

    def count(pred, t_s):
        def body(kb, c):
            r0 = pl.multiple_of(kb * tk, tk)
            return c + jnp.sum(jnp.where(pred(key_scr[pl.ds(r0, tk), :], t_s), 1.0, 0.0), axis=0, keepdims=True)
        return lax.fori_loop(0, qi + 1, body, jnp.zeros((1, tq), F32))

    def bit_step(i, t_u):
        cand_u = t_u | lax.shift_left(jnp.int32(1), 31 - i)
        cand_s = cand_u ^ INT_MIN
        c = count(lambda k, t: k >= t, cand_s) + jnp.where(cand_s <= KEY_OF_NEG_INF, n_beyond, 0.0)
        return jnp.where(c >= top_k, cand_u, t_u)
    thr = lax.fori_loop(0, 32, bit_step, jnp.zeros((1, tq), I32)) ^ INT_MIN

    n_above = count(lambda k, t: k > t, thr) + jnp.where(thr < KEY_OF_NEG_INF, n_beyond, 0.0)
    n_ties_kept = top_k - n_above

    strict_lower = jnp.where(row > col, 1.0, 0.0).astype(BF16)

    def make_bias(kb, ties_before):
        r0 = pl.multiple_of(kb * tk, tk)
        kblk = key_scr[pl.ds(r0, tk), :]
        eq = kblk == thr
        eq_f = jnp.where(eq, 1.0, 0.0)
        rank = jnp.dot(strict_lower, eq_f.astype(BF16), preferred_element_type=F32) + ties_before
        bias = jnp.where(kblk > thr, 0.0, jnp.where(eq, jnp.where(rank < n_ties_kept, 0.0, NEG_INF), NEG_INF))
        key_scr[pl.ds(r0, tk), :] = pltpu.bitcast(bias, I32)
        return ties_before + jnp.sum(eq_f, axis=0, keepdims=True)
    lax.fori_loop(0, qi + 1, make_bias, jnp.zeros((1, tq), F32))

    for hp in range(n_pairs):
        qm = _split_pair(q_ref[0, 0, hp], lane)

        def block(kb, carry, hp=hp, qm=qm):
            r0 = pl.multiple_of(kb * tk, tk)
            kblk = k_ref[0, 0, hp, pl.ds(r0, tk), :]
            bias = pltpu.bitcast(key_scr[pl.ds(r0, tk), :], F32)
            out = []
            for hh in range(2):
                s_t = _dot_t(kblk, qm[hh]) + bias
                vt = vt_scr[hp, kb, hh * HEAD_DIM:(hh + 1) * HEAD_DIM, :]
                out.append(_flash_update(s_t, vt, *carry[hh]))
            return tuple(out)

        carry = lax.fori_loop(0, qi + 1, block, (_flash_init(tq), _flash_init(tq)))
        o_ref[0, hp] = _finish_pair(carry[0], carry[1], g_ref[0, 0, hp])


def _dsa(p0, ik2, small, *, tq, top_k):
    _, B, HP, S, _ = p0.shape
    spec_q = lambda t: pl.BlockSpec((1, 1, HP, tq, LANES), lambda b, q, t=t: (t, b, 0, q, 0))
    spec_s = lambda t: pl.BlockSpec((1, 1, HP, S, LANES), lambda b, q, t=t: (t, b, 0, 0, 0))
    return pl.pallas_call(
        functools.partial(_dsa_kernel, tq=tq, top_k=top_k),
        grid=(B, S // tq),
        in_specs=[spec_q(T_DQ), spec_s(T_DK), spec_s(T_DV), spec_q(T_IQ), spec_q(T_DG),
                  pl.BlockSpec((1, S, LANES), lambda b, q: (b, 0, 0)),
                  pl.BlockSpec((1, tq, LANES), lambda b, q: (b, q, 0))],
        out_specs=pl.BlockSpec((1, HP, tq, LANES), lambda b, q: (b, 0, q, 0)),
        out_shape=jax.ShapeDtypeStruct((B, HP, S, LANES), BF16),
        scratch_shapes=[pltpu.VMEM((HP, S // tq, LANES, tq), BF16), pltpu.VMEM((S, tq), I32)],
        compiler_params=_cparams(2),
        name="dsa_attn",
    )(p0, p0, p0, p0, p0, ik2, small)


def _chunk_kernel(q_ref, k_ref, v_ref, g_ref, bt_ref, o_ref, vt_scr, *, tq):
    qi = pl.program_id(2)
    tk = tq
    n_blocks = k_ref.shape[3] // tk
    band_blocks = bt_ref.shape[1] // tk
    lane = lax.broadcasted_iota(I32, (tq, LANES), 1)

    @pl.when(qi == 0)
    def _prepare():
        def body(c, _):
            r0 = pl.multiple_of(c * tk, tk)
            vt_scr[c] = v_ref[0, 0, 0, pl.ds(r0, tk), :].astype(F32).T.astype(BF16)
            return 0
        lax.fori_loop(0, n_blocks, body, 0)

    qm = _split_pair(q_ref[0, 0, 0], lane)

    def block(jb, carry):
        kb = qi - (band_blocks - 1) + jb
        r0 = pl.multiple_of(kb * tk, tk)
        b0 = pl.multiple_of(jb * tk, tk)
        kblk = k_ref[0, 0, 0, pl.ds(r0, tk), :]
        out = []
        for hh in range(2):
            s_t = _dot_t(kblk, qm[hh]) + bt_ref[hh, pl.ds(b0, tk), :]
            vt = vt_scr[kb, hh * HEAD_DIM:(hh + 1) * HEAD_DIM, :]
            out.append(_flash_update(s_t, vt, *carry[hh]))
        return tuple(out)

    first = jnp.maximum(band_blocks - 1 - qi, 0)
    carry = lax.fori_loop(first, band_blocks, block, (_flash_init(tq), _flash_init(tq)))
    o_ref[0, 0] = _finish_pair(carry[0], carry[1], g_ref[0, 0, 0])


def _chunk_attn(p1, bias_t, *, tq):
    _, B, HP, S, _ = p1.shape
    band = bias_t.shape[1]
    spec_q = lambda t: pl.BlockSpec((1, 1, 1, tq, LANES), lambda b, h, q, t=t: (t, b, h, q, 0))
    spec_s = lambda t: pl.BlockSpec((1, 1, 1, S, LANES), lambda b, h, q, t=t: (t, b, h, 0, 0))
    return pl.pallas_call(
        functools.partial(_chunk_kernel, tq=tq),
        grid=(B, HP, S // tq),
        in_specs=[spec_q(0), spec_s(1), spec_s(2), spec_q(3),
                  pl.BlockSpec((2, band, tq), lambda b, h, q: (h, 0, 0))],
        out_specs=pl.BlockSpec((1, 1, tq, LANES), lambda b, h, q: (b, h, q, 0)),
        out_shape=jax.ShapeDtypeStruct((B, HP, S, LANES), BF16),
        scratch_shapes=[pltpu.VMEM((S // tq, LANES, tq), BF16)],
        compiler_params=_cparams(3),
        name="chunk_attn",
    )(p1, p1, p1, p1, bias_t)


def _band_bias_t(rel_table, *, tq):
    left = CHK_LEFT_CHUNKS * CHUNK
    pad = -(-left // tq) * tq
    r = jnp.arange(pad + tq)[:, None]
    c = jnp.arange(tq)[None, :]
    dist = c + pad - r
    bucket = jnp.clip(dist, -MAX_REL_DIST, MAX_REL_DIST) + MAX_REL_DIST
    back = (c + pad) // CHUNK - r // CHUNK
    in_band = (back >= 0) & (back <= CHK_LEFT_CHUNKS)
    bias = rel_table.astype(F32)[:, bucket]
    return jnp.where(in_band[None], bias, NEG_INF)


def _out_kernel(*refs, n_in):
    a_refs = refs[:n_in]
    w_ref, x_ref, g_ref, o_ref = refs[n_in:]
    parts = []
    for a in a_refs:
        parts.extend(a[0, hp] for hp in range(a.shape[1]))
    att = jnp.concatenate(parts, axis=-1)
    y = jnp.dot(att, w_ref[...], preferred_element_type=F32)
    ms = jnp.mean(y * y, axis=-1, keepdims=True)
    o_ref[0] = x_ref[0] + (y * lax.rsqrt(ms + RMS_EPS)) * g_ref[...]


def _out_proj(atts, w, x, g, *, tm):
    B, S, D = x.shape
    n_in = len(atts)
    a_specs = [pl.BlockSpec((1, a.shape[1], tm, LANES), lambda b, s: (b, 0, s, 0)) for a in atts]
    return pl.pallas_call(
        functools.partial(_out_kernel, n_in=n_in),
        grid=(B, S // tm),
        in_specs=a_specs + [
            pl.BlockSpec(w.shape, lambda b, s: (0, 0)),
            pl.BlockSpec((1, tm, D), lambda b, s: (b, s, 0)),
            pl.BlockSpec((1, D), lambda b, s: (0, 0)),
        ],
        out_specs=pl.BlockSpec((1, tm, D), lambda b, s: (b, s, 0)),
        out_shape=jax.ShapeDtypeStruct((B, S, D), F32),
        compiler_params=_cparams(2),
        name=f"out_proj{n_in}",
    )(*atts, w, x, g)


def _rope_tiles(positions):
    inv_freq = ROPE_THETA ** (-jnp.arange(0, ROT_DIM, 2, dtype=F32) / ROT_DIM)
    ang = positions.astype(F32)[..., None] * inv_freq
    cos, sin = jnp.cos(ang), jnp.sin(ang)
    half = ROT_DIM // 2
    one = jnp.ones(ang.shape[:-1] + (HEAD_DIM - ROT_DIM,), F32)
    cos_head = jnp.concatenate([cos, cos, one], axis=-1)
    sin_head = jnp.concatenate([-sin, sin, 0.0 * one], axis=-1)
    assert cos_head.shape[-1] == HEAD_DIM and half * 2 == ROT_DIM
    return jnp.tile(cos_head, (1, 1, 2)), jnp.tile(sin_head, (1, 1, 2))


def _layer0_weights(w_in, b_forget, idx_k_g, idx_k_b):
    scale = HEAD_DIM ** -0.5
    off = 0
    cols = {}
    for name, n in (("fq", WIDTH), ("fk", WIDTH), ("fv", WIDTH), ("fl", FOX_HEADS), ("fg", WIDTH),
                    ("dq", WIDTH), ("dk", WIDTH), ("dv", WIDTH), ("iq", WIDTH), ("ik", IDX_DIM),
                    ("iw", IDX_HEADS), ("dg", WIDTH)):
        cols[name] = w_in[:, off:off + n]
        off += n
    assert off == w_in.shape[1]
    order = [None] * 9
    order[T_DQ], order[T_DK], order[T_IQ] = cols["dq"] * scale, cols["dk"], cols["iq"]
    order[T_FQ], order[T_FK], order[T_FV], order[T_FG] = cols["fq"] * scale, cols["fk"], cols["fv"], cols["fg"]
    order[T_DV], order[T_DG] = cols["dv"], cols["dg"]
    w_main = jnp.concatenate(order, axis=1).astype(BF16)
    pad = jnp.zeros((w_in.shape[0], LANES - L_IW - IDX_HEADS), w_in.dtype)
    w_small = jnp.concatenate([cols["ik"], cols["fl"], cols["iw"], pad], axis=1).astype(BF16)
    par = jnp.zeros((8, LANES), F32)
    par = par.at[0, L_FL:L_FL + FOX_HEADS].set(b_forget.astype(F32))
    par = par.at[1, :IDX_DIM].set(idx_k_g.astype(F32))
    par = par.at[2, :IDX_DIM].set(idx_k_b.astype(F32))
    return w_main, w_small, par


def _even_layer(x, cos_t, sin_t, pre_g, post_g, w_in, b_forget, idx_k_g, idx_k_b, w_out, top_k, cfg):
    w_main, w_small, par = _layer0_weights(w_in, b_forget, idx_k_g, idx_k_b)
    p0, ik2, small = _proj0(x, pre_g[None].astype(F32), w_main, w_small, cos_t, sin_t, par, tm=cfg["tm_proj"])
    fox = _fox(p0, small, tq=cfg["tq"])
    dsa = _dsa(p0, ik2, small, tq=cfg["tq"], top_k=top_k)
    return _out_proj([fox, dsa], w_out.astype(BF16), x, post_g[None].astype(F32), tm=cfg["tm_out"])


def _odd_layer(x, pre_g, post_g, w_in, rel_table, w_out, cfg):
    scale = HEAD_DIM ** -0.5
    width = CHK_HEADS * HEAD_DIM
    w = jnp.concatenate([w_in[:, :width] * scale, w_in[:, width:]], axis=1).astype(BF16)
    p1 = _proj1(x, pre_g[None].astype(F32), w, tm=cfg["tm_proj"], n_tensors=4)
    att = _chunk_attn(p1, _band_bias_t(rel_table, tq=cfg["tq"]), tq=cfg["tq"])
    return _out_proj([att], w_out.astype(BF16), x, post_g[None].astype(F32), tm=cfg["tm_out"])


def _config(seq):
    tq = 256
    assert seq % tq == 0 and tq % CHUNK == 0
    tm_proj = min(1024, seq)
    tm_out = min(512, seq)
    assert seq % tm_proj == 0 and seq % tm_out == 0
    return {"tq": tq, "tm_proj": tm_proj, "tm_out": tm_out}


def kernel(x, positions, pre_norm_g, post_norm_g, w_in_even, b_forget, idx_k_g, idx_k_b,
           w_out_even, w_in_odd, rel_bias, w_out_odd):
    seq = x.shape[1]
    depth = pre_norm_g.shape[0]
    cfg = _config(seq)
    top_k = min(DSA_TOPK, seq // 4)
    cos_t, sin_t = _rope_tiles(positions)
    for layer in range(depth):
        i = layer // 2
        if layer % 2 == 0:
            x = _even_layer(x, cos_t, sin_t, pre_norm_g[layer], post_norm_g[layer], w_in_even[i],
                            b_forget[i], idx_k_g[i], idx_k_b[i], w_out_even[i], top_k, cfg)
        else:
            x = _odd_layer(x, pre_norm_g[layer], post_norm_g[layer], w_in_odd[i], rel_bias[i], w_out_odd[i], cfg)
    return x
```

```python
import functools

import numpy as np
import jax
import jax.numpy as jnp
from jax import lax
from jax.experimental import pallas as pl
from jax.experimental.pallas import tpu as pltpu

F32 = jnp.float32
BF16 = jnp.bfloat16
I32 = jnp.int32

LANES = 128
HEAD_DIM = 64
HALF = HEAD_DIM
ROT_DIM = HEAD_DIM // 4
ROPE_THETA = 500000.0
RMS_EPS = 1e-6
LN_EPS = 1e-6
NEG_INF = -1e30
CHUNK = 64
CHUNK_SHIFT = 6
FOX_HEADS = 8
DSA_HEADS = 8
IDX_HEADS = 8
IDX_DIM = 64
DSA_TOPK = 256
CHK_HEADS = 16
CHK_LEFT_CHUNKS = 8
MAX_REL_DIST = 128

WIDTH = 512
T_DQ, T_DK, T_IQ, T_FQ, T_FK, T_FV, T_FG, T_DV, T_DG = range(9)
N_ROPE_TILES = 3
L_IK = 0
L_FL = 64
L_IW = 72
L_AUG = 64

VMEM_LIMIT = 52 * 1024 * 1024

INT_MIN = -2147483648


def _sortable_key_of(value):
    b = int(np.array(value, np.float32).view(np.int32))
    return b ^ ((b >> 31) & 0x7FFFFFFF)


KEY_OF_NEG_INF = _sortable_key_of(NEG_INF)


def _cparams(n_axes):
    return pltpu.CompilerParams(dimension_semantics=("arbitrary",) * n_axes,
                                vmem_limit_bytes=VMEM_LIMIT)


def _dot(a, b):
    return jnp.dot(a, b, preferred_element_type=F32)


def _block_start(index, size):
    return index * size if isinstance(index, int) else pl.multiple_of(index * size, size)


def _rope(t, cos_t, sin_t, lane):
    first = (lane & (HALF - 1)) < (ROT_DIM // 2)
    partner = jnp.where(first, pltpu.roll(t, LANES - ROT_DIM // 2, 1), pltpu.roll(t, ROT_DIM // 2, 1))
    return t * cos_t + partner * sin_t


def _split3(x):
    hi = x.astype(BF16)
    r1 = x - hi.astype(F32)
    mid = r1.astype(BF16)
    lo = (r1 - mid.astype(F32)).astype(BF16)
    return hi, mid, lo


def _proj0_kernel(x_ref, g_ref, w_ref, ws_ref, cos_ref, sin_ref, par_ref,
                  p_ref, ik_ref, sm_ref, h_scr, carry_scr, *, tm):
    s = pl.program_id(1)
    j = pl.program_id(2)
    lane = lax.broadcasted_iota(I32, (tm, LANES), 1)

    @pl.when(j == 0)
    def _prologue():
        xf = x_ref[0]
        ms = jnp.mean(xf * xf, axis=-1, keepdims=True)
        hb = ((xf * lax.rsqrt(ms + RMS_EPS)) * g_ref[...]).astype(BF16)
        h_scr[...] = hb
        small = jnp.dot(hb, ws_ref[...], preferred_element_type=F32)

        is_ik = lane < IDX_DIM
        mu = jnp.sum(jnp.where(is_ik, small, 0.0), axis=-1, keepdims=True) * (1.0 / IDX_DIM)
        xc = small - mu
        var = jnp.sum(jnp.where(is_ik, xc * xc, 0.0), axis=-1, keepdims=True) * (1.0 / IDX_DIM)
        y = xc * lax.rsqrt(var + LN_EPS) * par_ref[1:2, :] + par_ref[2:3, :]
        yr = _rope(y, cos_ref[0], sin_ref[0], lane)
        ik_ref[0] = jnp.where(is_ik, yr, pltpu.roll(yr, HALF, 1)).astype(BF16)

        z = small + par_ref[0:1, :]
        ls = jnp.minimum(z, 0.0) - jnp.log1p(jnp.exp(-jnp.abs(z)))
        is_f = (lane >= L_FL) & (lane < L_FL + FOX_HEADS)
        ls = jnp.where(is_f, ls, 0.0)
        is_iw = (lane >= L_IW) & (lane < L_IW + IDX_HEADS)
        other = small * jnp.where(is_iw, float((IDX_HEADS * IDX_DIM) ** -0.5), 1.0)

        @pl.when(s == 0)
        def _():
            carry_scr[...] = jnp.zeros_like(carry_scr)

        blk = LANES
        tri = (lax.broadcasted_iota(I32, (blk, blk), 0) >= lax.broadcasted_iota(I32, (blk, blk), 1))
        tri = jnp.where(tri, 1.0, 0.0).astype(BF16)
        carry = carry_scr[0:1, :]
        lane_blk = lax.broadcasted_iota(I32, (blk, LANES), 1)
        is_f_blk = (lane_blk >= L_FL) & (lane_blk < L_FL + FOX_HEADS)
        for r in range(tm // blk):
            hi, mid, lo = _split3(ls[r * blk:(r + 1) * blk])
            cum = (jnp.dot(tri, hi, preferred_element_type=F32)
                   + jnp.dot(tri, mid, preferred_element_type=F32)
                   + jnp.dot(tri, lo, preferred_element_type=F32)) + carry
            sm_ref[0, r * blk:(r + 1) * blk, :] = jnp.where(is_f_blk, cum, other[r * blk:(r + 1) * blk])
            carry = cum[blk - 1:blk, :]
        carry_scr[0:1, :] = carry

    res = jnp.dot(h_scr[...], w_ref[...], preferred_element_type=F32)

    @pl.when(j < N_ROPE_TILES)
    def _with_rope():
        c = cos_ref[0]
        sn = sin_ref[0]
        for hp in range(WIDTH // LANES):
            p_ref[0, 0, hp] = _rope(res[:, hp * LANES:(hp + 1) * LANES], c, sn, lane).astype(BF16)

    @pl.when(j >= N_ROPE_TILES)
    def _plain():
        for hp in range(WIDTH // LANES):
            p_ref[0, 0, hp] = res[:, hp * LANES:(hp + 1) * LANES].astype(BF16)


def _proj0(x, g, w_main, w_small, cos_t, sin_t, par, *, tm):
    B, S, D = x.shape
    n_tiles = w_main.shape[1] // WIDTH
    hp = WIDTH // LANES
    return pl.pallas_call(
        functools.partial(_proj0_kernel, tm=tm),
        grid=(B, S // tm, n_tiles),
        in_specs=[
            pl.BlockSpec((1, tm, D), lambda b, s, j: (b, s, 0)),
            pl.BlockSpec((1, D), lambda b, s, j: (0, 0)),
            pl.BlockSpec((D, WIDTH), lambda b, s, j: (0, j)),
            pl.BlockSpec((D, LANES), lambda b, s, j: (0, 0)),
            pl.BlockSpec((1, tm, LANES), lambda b, s, j: (b, s, 0)),
            pl.BlockSpec((1, tm, LANES), lambda b, s, j: (b, s, 0)),
            pl.BlockSpec((8, LANES), lambda b, s, j: (0, 0)),
        ],
        out_specs=[
            pl.BlockSpec((1, 1, hp, tm, LANES), lambda b, s, j: (j, b, 0, s, 0)),
            pl.BlockSpec((1, tm, LANES), lambda b, s, j: (b, s, 0)),
            pl.BlockSpec((1, tm, LANES), lambda b, s, j: (b, s, 0)),
        ],
        out_shape=[
            jax.ShapeDtypeStruct((n_tiles, B, hp, S, LANES), BF16),
            jax.ShapeDtypeStruct((B, S, LANES), BF16),
            jax.ShapeDtypeStruct((B, S, LANES), F32),
        ],
        scratch_shapes=[pltpu.VMEM((tm, D), BF16), pltpu.VMEM((8, LANES), F32)],
        compiler_params=_cparams(3),
        name="proj0",
    )(x, g, w_main, w_small, cos_t, sin_t, par)


def _proj1_kernel(x_ref, g_ref, w_ref, p_ref, h_scr):
    j = pl.program_id(2)

    @pl.when(j == 0)
    def _prologue():
        xf = x_ref[0]
        ms = jnp.mean(xf * xf, axis=-1, keepdims=True)
        h_scr[...] = ((xf * lax.rsqrt(ms + RMS_EPS)) * g_ref[...]).astype(BF16)

    res = jnp.dot(h_scr[...], w_ref[...], preferred_element_type=F32)
    for hp in range(WIDTH // LANES):
        p_ref[0, 0, hp] = res[:, hp * LANES:(hp + 1) * LANES].astype(BF16)


def _proj1(x, g, w, *, tm, n_tensors):
    B, S, D = x.shape
    n_tiles = w.shape[1] // WIDTH
    per = n_tiles // n_tensors
    hp = WIDTH // LANES
    return pl.pallas_call(
        _proj1_kernel,
        grid=(B, S // tm, n_tiles),
        in_specs=[
            pl.BlockSpec((1, tm, D), lambda b, s, j: (b, s, 0)),
            pl.BlockSpec((1, D), lambda b, s, j: (0, 0)),
            pl.BlockSpec((D, WIDTH), lambda b, s, j: (0, j)),
        ],
        out_specs=pl.BlockSpec((1, 1, hp, tm, LANES), lambda b, s, j: (j // per, b, j % per, s, 0)),
        out_shape=jax.ShapeDtypeStruct((n_tensors, B, per * hp, S, LANES), BF16),
        scratch_shapes=[pltpu.VMEM((tm, D), BF16)],
        compiler_params=_cparams(3),
        name="proj1",
    )(x, g, w)


def _flash_scratch(n_heads, tq):
    return [pltpu.VMEM((n_heads, tq, tq), F32), pltpu.VMEM((n_heads, 1, tq), F32),
            pltpu.VMEM((n_heads, 1, tq), F32), pltpu.VMEM((n_heads, HEAD_DIM, tq), F32)]


def _flash_update(h, s_t, vt, m_scr, l_scr, acc_scr):
    m = m_scr[h]
    m_new = jnp.maximum(m, jnp.max(s_t, axis=0, keepdims=True))
    alpha = jnp.exp(m - m_new)
    p = jnp.exp(s_t - m_new)
    m_scr[h] = m_new
    l_scr[h] = alpha * l_scr[h] + jnp.sum(p, axis=0, keepdims=True)
    acc_scr[h] = alpha * acc_scr[h] + _dot(vt, p.astype(BF16))


def _flash_attend(n_heads, lo, hi, scores_fn, values_fn, scratch, last_scores_fixup=None):
    s_scr, m_scr, l_scr, acc_scr = scratch
    m_scr[...] = jnp.full(m_scr.shape, NEG_INF, F32)
    l_scr[...] = jnp.zeros(l_scr.shape, F32)
    acc_scr[...] = jnp.zeros(acc_scr.shape, F32)
    for h in range(n_heads):
        s_scr[h] = scores_fn(lo, h)

    def body(kb, _):
        for h in range(n_heads):
            s_t = s_scr[h]
            s_scr[h] = scores_fn(kb + 1, h)
            _flash_update(h, s_t, values_fn(kb, h), m_scr, l_scr, acc_scr)
        return 0
    lax.fori_loop(lo, hi - 1, body, 0)

    for h in range(n_heads):
        s_t = s_scr[h]
        if last_scores_fixup is not None:
            s_t = last_scores_fixup(s_t)
        _flash_update(h, s_t, values_fn(hi - 1, h), m_scr, l_scr, acc_scr)


def _finish_pair(hp, gate, l_scr, acc_scr):
    acc_t = jnp.concatenate([acc_scr[2 * hp] / l_scr[2 * hp], acc_scr[2 * hp + 1] / l_scr[2 * hp + 1]], axis=0)
    g = gate.astype(F32)
    return (acc_t.T * (g * jax.nn.sigmoid(g))).astype(BF16)


def _split_pair(q, lane):
    qf = q.astype(F32)
    lo = lane < HALF
    return jnp.where(lo, qf, 0.0).T.astype(BF16), jnp.where(lo, 0.0, qf).T.astype(BF16)


def _fox_aug(base, f_col, lane, is_key):
    hi, mid, lo = (t.astype(F32) for t in _split3(f_col))
    if is_key:
        a, b, c, d = -hi, -mid, -lo, 1.0
        tail = jnp.where(lane < L_AUG + 6, d, 0.0)
        aug = jnp.where(lane == L_AUG, a, jnp.where(lane == L_AUG + 1, b, jnp.where(lane == L_AUG + 2, c, tail)))
    else:
        tail = jnp.where(lane == L_AUG + 3, hi, jnp.where(lane == L_AUG + 4, mid,
                                                         jnp.where(lane == L_AUG + 5, lo, 0.0)))
        aug = jnp.where(lane < L_AUG + 3, 1.0, tail)
    out = jnp.where(lane < L_AUG, base, aug)
    return out.astype(BF16) if is_key else out.T.astype(BF16)


def _fox_kernel(q_ref, k_ref, v_ref, g_ref, f_ref, o_ref, ka_scr, vt_scr, *flash_scr, tq):
    qi = pl.program_id(1)
    tk = tq
    n_pairs = k_ref.shape[2]
    n_blocks = k_ref.shape[3] // tk
    lane = lax.broadcasted_iota(I32, (tq, LANES), 1)

    def head_f(f_tile, h):
        return jnp.sum(jnp.where(lane == L_FL + h, f_tile, 0.0), axis=-1, keepdims=True)

    def head_base(t, hh):
        return t if hh == 0 else pltpu.roll(t, HALF, 1)

    @pl.when(qi == 0)
    def _prepare():
        def body(c, _):
            r0 = pl.multiple_of(c * tk, tk)
            ff = f_ref[0, pl.ds(r0, tk), :]
            for hp in range(n_pairs):
                kf = k_ref[0, 0, hp, pl.ds(r0, tk), :].astype(F32)
                for hh in range(2):
                    h = 2 * hp + hh
                    ka_scr[h, pl.ds(r0, tk), :] = _fox_aug(head_base(kf, hh), head_f(ff, h), lane, True)
                vt_scr[hp, c] = v_ref[0, 0, hp, pl.ds(r0, tk), :].astype(F32).T.astype(BF16)
            return 0
        lax.fori_loop(0, n_blocks, body, 0)

    q0 = pl.multiple_of(qi * tq, tq)
    fq = f_ref[0, pl.ds(q0, tq), :]
    qa = []
    for hp in range(n_pairs):
        qf = q_ref[0, 0, hp].astype(F32)
        qa.extend(_fox_aug(head_base(qf, hh), head_f(fq, 2 * hp + hh), lane, False) for hh in range(2))

    def scores(kb, h):
        return _dot(ka_scr[h, pl.ds(_block_start(kb, tk), tk), :], qa[h])

    def values(kb, h):
        return vt_scr[h // 2, kb, (h % 2) * HEAD_DIM:(h % 2 + 1) * HEAD_DIM, :]

    def causal(s_t):
        future = lax.broadcasted_iota(I32, (tk, tq), 0) > lax.broadcasted_iota(I32, (tk, tq), 1)
        return jnp.where(future, NEG_INF, s_t)

    _flash_attend(2 * n_pairs, 0, qi + 1, scores, values, flash_scr, last_scores_fixup=causal)
    for hp in range(n_pairs):
        o_ref[0, hp] = _finish_pair(hp, g_ref[0, 0, hp], flash_scr[2], flash_scr[3])


def _fox(p0, small, *, tq):
    _, B, HP, S, _ = p0.shape
    spec_q = lambda t: pl.BlockSpec((1, 1, HP, tq, LANES), lambda b, q, t=t: (t, b, 0, q, 0))
    spec_s = lambda t: pl.BlockSpec((1, 1, HP, S, LANES), lambda b, q, t=t: (t, b, 0, 0, 0),
                                    pipeline_mode=pl.Buffered(1))
    return pl.pallas_call(
        functools.partial(_fox_kernel, tq=tq),
        grid=(B, S // tq),
        in_specs=[spec_q(T_FQ), spec_s(T_FK), spec_s(T_FV), spec_q(T_FG),
                  pl.BlockSpec((1, S, LANES), lambda b, q: (b, 0, 0), pipeline_mode=pl.Buffered(1))],
        out_specs=pl.BlockSpec((1, HP, tq, LANES), lambda b, q: (b, 0, q, 0)),
        out_shape=jax.ShapeDtypeStruct((B, HP, S, LANES), BF16),
        scratch_shapes=[pltpu.VMEM((2 * HP, S, LANES), BF16), pltpu.VMEM((HP, S // tq, LANES, tq), BF16)]
        + _flash_scratch(2 * HP, tq),
        compiler_params=_cparams(2),
        name="fox_attn",
    )(p0, p0, p0, p0, small)


def _dsa_kernel(q_ref, k_ref, v_ref, iq_ref, g_ref, ik_ref, sm_ref, o_ref, vt_scr, key_scr,
                *flash_scr, tq, top_k):
    qi = pl.program_id(1)
    tk = tq
    n_pairs = k_ref.shape[2]
    seq = k_ref.shape[3]
    n_blocks = seq // tk
    lane = lax.broadcasted_iota(I32, (tq, LANES), 1)

    @pl.when(qi == 0)
    def _prepare():
        def body(c, _):
            r0 = pl.multiple_of(c * tk, tk)
            for hp in range(n_pairs):
                vt_scr[hp, c] = v_ref[0, 0, hp, pl.ds(r0, tk), :].astype(F32).T.astype(BF16)
            return 0
        lax.fori_loop(0, n_blocks, body, 0)

    iw_t = sm_ref[0].T[L_IW:L_IW + IDX_HEADS, :]
    iq_m = []
    for hp in range(IDX_HEADS // 2):
        iq_m.extend(_split_pair(iq_ref[0, 0, hp], lane))

    def index_keys(r0):
        ikb = ik_ref[0, pl.ds(r0, tk), :]
        score = jnp.zeros((tk, tq), F32)
        for h in range(IDX_HEADS):
            score = score + jnp.maximum(_dot(ikb, iq_m[h]), 0.0) * iw_t[h:h + 1, :]
        bits = pltpu.bitcast(score, I32)
        return bits ^ ((bits >> 31) & 0x7FFFFFFF)

    def fill(kb, _):
        r0 = pl.multiple_of(kb * tk, tk)
        key_scr[pl.ds(r0, tk), :] = index_keys(r0)
        return 0
    lax.fori_loop(0, qi, fill, 0)

    d0 = pl.multiple_of(qi * tk, tk)
    row = lax.broadcasted_iota(I32, (tk, tq), 0)
    col = lax.broadcasted_iota(I32, (tk, tq), 1)
    beyond_chunk = (row >> CHUNK_SHIFT) > (col >> CHUNK_SHIFT)
    key_scr[pl.ds(d0, tk), :] = jnp.where(beyond_chunk, INT_MIN, index_keys(d0))

    q_pos = qi * tq + lax.broadcasted_iota(I32, (1, tq), 1)
    n_beyond = (seq - ((q_pos >> CHUNK_SHIFT) + 1) * CHUNK).astype(F32)

    def count(pred, t_s):
        def body(kb, c):
            r0 = pl.multiple_of(kb * tk, tk)
            return c + jnp.sum(jnp.where(pred(key_scr[pl.ds(r0, tk), :], t_s), 1.0, 0.0), axis=0, keepdims=True)
        return lax.fori_loop(0, qi + 1, body, jnp.zeros((1, tq), F32))

    def bit_step(i, t_u):
        cand_u = t_u | lax.shift_left(jnp.int32(1), 31 - i)
        cand_s = cand_u ^ INT_MIN
        c = count(lambda k, t: k >= t, cand_s) + jnp.where(cand_s <= KEY_OF_NEG_INF, n_beyond, 0.0)
        return jnp.where(c >= top_k, cand_u, t_u)
    thr = lax.fori_loop(0, 32, bit_step, jnp.zeros((1, tq), I32)) ^ INT_MIN

    n_above = count(lambda k, t: k > t, thr) + jnp.where(thr < KEY_OF_NEG_INF, n_beyond, 0.0)
    n_ties_kept = top_k - n_above

    strict_lower = jnp.where(row > col, 1.0, 0.0).astype(BF16)

    def make_bias(kb, ties_before):
        r0 = pl.multiple_of(kb * tk, tk)
        kblk = key_scr[pl.ds(r0, tk), :]
        eq = kblk == thr
        eq_f = jnp.where(eq, 1.0, 0.0)
        rank = jnp.dot(strict_lower, eq_f.astype(BF16), preferred_element_type=F32) + ties_before
        bias = jnp.where(kblk > thr, 0.0, jnp.where(eq, jnp.where(rank < n_ties_kept, 0.0, NEG_INF), NEG_INF))
        key_scr[pl.ds(r0, tk), :] = pltpu.bitcast(bias, I32)
        return ties_before + jnp.sum(eq_f, axis=0, keepdims=True)
    lax.fori_loop(0, qi + 1, make_bias, jnp.zeros((1, tq), F32))

    qm = []
    for hp in range(n_pairs):
        qm.extend(_split_pair(q_ref[0, 0, hp], lane))

    def scores(kb, h):
        r0 = _block_start(kb, tk)
        bias = pltpu.bitcast(key_scr[pl.ds(r0, tk), :], F32)
        return _dot(k_ref[0, 0, h // 2, pl.ds(r0, tk), :], qm[h]) + bias

    def values(kb, h):
        return vt_scr[h // 2, kb, (h % 2) * HEAD_DIM:(h % 2 + 1) * HEAD_DIM, :]

    _flash_attend(2 * n_pairs, 0, qi + 1, scores, values, flash_scr)
    for hp in range(n_pairs):
        o_ref[0, hp] = _finish_pair(hp, g_ref[0, 0, hp], flash_scr[2], flash_scr[3])


def _dsa(p0, ik2, small, *, tq, top_k):
    _, B, HP, S, _ = p0.shape
    spec_q = lambda t: pl.BlockSpec((1, 1, HP, tq, LANES), lambda b, q, t=t: (t, b, 0, q, 0))
    spec_s = lambda t: pl.BlockSpec((1, 1, HP, S, LANES), lambda b, q, t=t: (t, b, 0, 0, 0),
                                    pipeline_mode=pl.Buffered(1))
    return pl.pallas_call(
        functools.partial(_dsa_kernel, tq=tq, top_k=top_k),
        grid=(B, S // tq),
        in_specs=[spec_q(T_DQ), spec_s(T_DK), spec_s(T_DV), spec_q(T_IQ), spec_q(T_DG),
                  pl.BlockSpec((1, S, LANES), lambda b, q: (b, 0, 0), pipeline_mode=pl.Buffered(1)),
                  pl.BlockSpec((1, tq, LANES), lambda b, q: (b, q, 0))],
        out_specs=pl.BlockSpec((1, HP, tq, LANES), lambda b, q: (b, 0, q, 0)),
        out_shape=jax.ShapeDtypeStruct((B, HP, S, LANES), BF16),
        scratch_shapes=[pltpu.VMEM((HP, S // tq, LANES, tq), BF16), pltpu.VMEM((S, tq), I32)]
        + _flash_scratch(2 * HP, tq),
        compiler_params=_cparams(2),
        name="dsa_attn",
    )(p0, p0, p0, p0, p0, ik2, small)


def _chunk_kernel(q_ref, k_ref, v_ref, g_ref, bt_ref, o_ref, vt_scr, *flash_scr, tq):
    qi = pl.program_id(2)
    tk = tq
    n_pairs = k_ref.shape[2]
    n_blocks = k_ref.shape[3] // tk
    band_blocks = bt_ref.shape[1] // tk
    lane = lax.broadcasted_iota(I32, (tq, LANES), 1)

    @pl.when(qi == 0)
    def _prepare():
        def body(c, _):
            r0 = pl.multiple_of(c * tk, tk)
            for hp in range(n_pairs):
                vt_scr[hp, c] = v_ref[0, 0, hp, pl.ds(r0, tk), :].astype(F32).T.astype(BF16)
            return 0
        lax.fori_loop(0, n_blocks, body, 0)

    qm = []
    for hp in range(n_pairs):
        qm.extend(_split_pair(q_ref[0, 0, hp], lane))

    def scores(jb, h):
        r0 = _block_start(qi - (band_blocks - 1) + jb, tk)
        b0 = _block_start(jb, tk)
        return _dot(k_ref[0, 0, h // 2, pl.ds(r0, tk), :], qm[h]) + bt_ref[h, pl.ds(b0, tk), :]

    def values(jb, h):
        return vt_scr[h // 2, qi - (band_blocks - 1) + jb, (h % 2) * HEAD_DIM:(h % 2 + 1) * HEAD_DIM, :]

    first = jnp.maximum(band_blocks - 1 - qi, 0)
    _flash_attend(2 * n_pairs, first, band_blocks, scores, values, flash_scr)
    for hp in range(n_pairs):
        o_ref[0, hp] = _finish_pair(hp, g_ref[0, 0, hp], flash_scr[2], flash_scr[3])


def _chunk_attn(p1, bias_t, *, tq, pairs_per_step):
    _, B, HP, S, _ = p1.shape
    band = bias_t.shape[1]
    pp = pairs_per_step
    spec_q = lambda t: pl.BlockSpec((1, 1, pp, tq, LANES), lambda b, h, q, t=t: (t, b, h, q, 0))
    spec_s = lambda t: pl.BlockSpec((1, 1, pp, S, LANES), lambda b, h, q, t=t: (t, b, h, 0, 0),
                                    pipeline_mode=pl.Buffered(1))
    return pl.pallas_call(
        functools.partial(_chunk_kernel, tq=tq),
        grid=(B, HP // pp, S // tq),
        in_specs=[spec_q(0), spec_s(1), spec_s(2), spec_q(3),
                  pl.BlockSpec((2 * pp, band, tq), lambda b, h, q: (h, 0, 0), pipeline_mode=pl.Buffered(1))],
        out_specs=pl.BlockSpec((1, pp, tq, LANES), lambda b, h, q: (b, h, q, 0)),
        out_shape=jax.ShapeDtypeStruct((B, HP, S, LANES), BF16),
        scratch_shapes=[pltpu.VMEM((pp, S // tq, LANES, tq), BF16)] + _flash_scratch(2 * pp, tq),
        compiler_params=_cparams(3),
        name="chunk_attn",
    )(p1, p1, p1, p1, bias_t)


def _band_bias_t(rel_table, *, tq):
    left = CHK_LEFT_CHUNKS * CHUNK
    pad = -(-left // tq) * tq
    rows = pad + tq
    r = jnp.arange(rows)[:, None]
    c = jnp.arange(tq)[None, :]
    back = (c + pad) // CHUNK - r // CHUNK
    in_band = (back >= 0) & (back <= CHK_LEFT_CHUNKS)
    period = rows + tq
    j = jnp.arange(period)
    c_minus_r = jnp.where(j < tq, j, j - period)
    bucket = jnp.clip(c_minus_r + pad, -MAX_REL_DIST, MAX_REL_DIST) + MAX_REL_DIST
    per_offset = rel_table.astype(F32)[:, bucket]
    n_heads = rel_table.shape[0]
    bias = jnp.tile(per_offset, (1, rows))[:, :rows * (period - 1)].reshape(n_heads, rows, period - 1)[:, :, :tq]
    return jnp.where(in_band[None], bias, NEG_INF)


def _out_kernel(*refs, n_in):
    a_refs = refs[:n_in]
    w_ref, x_ref, g_ref, o_ref = refs[n_in:]
    parts = []
    for a in a_refs:
        parts.extend(a[0, hp] for hp in range(a.shape[1]))
    att = jnp.concatenate(parts, axis=-1)
    y = jnp.dot(att, w_ref[...], preferred_element_type=F32)
    ms = jnp.mean(y * y, axis=-1, keepdims=True)
    o_ref[0] = x_ref[0] + (y * lax.rsqrt(ms + RMS_EPS)) * g_ref[...]


def _out_proj(atts, w, x, g, *, tm):
    B, S, D = x.shape
    n_in = len(atts)
    a_specs = [pl.BlockSpec((1, a.shape[1], tm, LANES), lambda b, s: (b, 0, s, 0)) for a in atts]
    return pl.pallas_call(
        functools.partial(_out_kernel, n_in=n_in),
        grid=(B, S // tm),
        in_specs=a_specs + [
            pl.BlockSpec(w.shape, lambda b, s: (0, 0)),
            pl.BlockSpec((1, tm, D), lambda b, s: (b, s, 0)),
            pl.BlockSpec((1, D), lambda b, s: (0, 0)),
        ],
        out_specs=pl.BlockSpec((1, tm, D), lambda b, s: (b, s, 0)),
        out_shape=jax.ShapeDtypeStruct((B, S, D), F32),
        compiler_params=_cparams(2),
        name=f"out_proj{n_in}",
    )(*atts, w, x, g)


def _rope_tiles(positions):
    inv_freq = ROPE_THETA ** (-jnp.arange(0, ROT_DIM, 2, dtype=F32) / ROT_DIM)
    ang = positions.astype(F32)[..., None] * inv_freq
    cos, sin = jnp.cos(ang), jnp.sin(ang)
    half = ROT_DIM // 2
    one = jnp.ones(ang.shape[:-1] + (HEAD_DIM - ROT_DIM,), F32)
    cos_head = jnp.concatenate([cos, cos, one], axis=-1)
    sin_head = jnp.concatenate([-sin, sin, 0.0 * one], axis=-1)
    assert cos_head.shape[-1] == HEAD_DIM and half * 2 == ROT_DIM
    return jnp.tile(cos_head, (1, 1, 2)), jnp.tile(sin_head, (1, 1, 2))


def _layer0_weights(w_in, b_forget, idx_k_g, idx_k_b):
    scale = HEAD_DIM ** -0.5
    off = 0
    cols = {}
    for name, n in (("fq", WIDTH), ("fk", WIDTH), ("fv", WIDTH), ("fl", FOX_HEADS), ("fg", WIDTH),
                    ("dq", WIDTH), ("dk", WIDTH), ("dv", WIDTH), ("iq", WIDTH), ("ik", IDX_DIM),
                    ("iw", IDX_HEADS), ("dg", WIDTH)):
        cols[name] = w_in[:, off:off + n]
        off += n
    assert off == w_in.shape[1]
    order = [None] * 9
    order[T_DQ], order[T_DK], order[T_IQ] = cols["dq"] * scale, cols["dk"], cols["iq"]
    order[T_FQ], order[T_FK], order[T_FV], order[T_FG] = cols["fq"] * scale, cols["fk"], cols["fv"], cols["fg"]
    order[T_DV], order[T_DG] = cols["dv"], cols["dg"]
    w_main = jnp.concatenate(order, axis=1).astype(BF16)
    pad = jnp.zeros((w_in.shape[0], LANES - L_IW - IDX_HEADS), w_in.dtype)
    w_small = jnp.concatenate([cols["ik"], cols["fl"], cols["iw"], pad], axis=1).astype(BF16)
    par = jnp.zeros((8, LANES), F32)
    par = par.at[0, L_FL:L_FL + FOX_HEADS].set(b_forget.astype(F32))
    par = par.at[1, :IDX_DIM].set(idx_k_g.astype(F32))
    par = par.at[2, :IDX_DIM].set(idx_k_b.astype(F32))
    return w_main, w_small, par


def _even_layer(x, cos_t, sin_t, pre_g, post_g, w_in, b_forget, idx_k_g, idx_k_b, w_out, top_k, cfg):
    w_main, w_small, par = _layer0_weights(w_in, b_forget, idx_k_g, idx_k_b)
    p0, ik2, small = _proj0(x, pre_g[None].astype(F32), w_main, w_small, cos_t, sin_t, par, tm=cfg["tm_proj"])
    fox = _fox(p0, small, tq=cfg["tq"])
    dsa = _dsa(p0, ik2, small, tq=cfg["tq"], top_k=top_k)
    return _out_proj([fox, dsa], w_out.astype(BF16), x, post_g[None].astype(F32), tm=cfg["tm_out"])


def _odd_layer(x, pre_g, post_g, w_in, rel_table, w_out, cfg):
    scale = HEAD_DIM ** -0.5
    width = CHK_HEADS * HEAD_DIM
    w = jnp.concatenate([w_in[:, :width] * scale, w_in[:, width:]], axis=1).astype(BF16)
    p1 = _proj1(x, pre_g[None].astype(F32), w, tm=cfg["tm_proj"], n_tensors=4)
    att = _chunk_attn(p1, _band_bias_t(rel_table, tq=cfg["tq"]), tq=cfg["tq"], pairs_per_step=4)
    return _out_proj([att], w_out.astype(BF16), x, post_g[None].astype(F32), tm=cfg["tm_out"])


def _config(seq):
    tq = 256
    assert seq % tq == 0 and tq % CHUNK == 0
    tm_proj = min(1024, seq)
    tm_out = min(512, seq)
    assert seq % tm_proj == 0 and seq % tm_out == 0
    return {"tq": tq, "tm_proj": tm_proj, "tm_out": tm_out}


def kernel(x, positions, pre_norm_g, post_norm_g, w_in_even, b_forget, idx_k_g, idx_k_b,
           w_out_even, w_in_odd, rel_bias, w_out_odd):
    seq = x.shape[1]
    depth = pre_norm_g.shape[0]
    cfg = _config(seq)
    top_k = min(DSA_TOPK, seq // 4)
    cos_t, sin_t = _rope_tiles(positions)
    for layer in range(depth):
        i = layer // 2
        if layer % 2 == 0:
            x = _even_layer(x, cos_t, sin_t, pre_norm_g[layer], post_norm_g[layer], w_in_even[i],
                            b_forget[i], idx_k_g[i], idx_k_b[i], w_out_even[i], top_k, cfg)
        else:
            x = _odd_layer(x, pre_norm_g[layer], post_norm_g[layer], w_in_odd[i], rel_bias[i], w_out_odd[i], cfg)
    return x
```

```python
import functools

import numpy as np
import jax
import jax.numpy as jnp
from jax import lax
from jax.experimental import pallas as pl
from jax.experimental.pallas import tpu as pltpu

F32 = jnp.float32
BF16 = jnp.bfloat16
I32 = jnp.int32

LANES = 128
HEAD_DIM = 64
HALF = HEAD_DIM
ROT_DIM = HEAD_DIM // 4
ROPE_THETA = 500000.0
RMS_EPS = 1e-6
LN_EPS = 1e-6
NEG_INF = -1e30
LOG2E = 1.4426950408889634
CHUNK = 64
CHUNK_SHIFT = 6
FOX_HEADS = 8
DSA_HEADS = 8
IDX_HEADS = 8
IDX_DIM = 64
DSA_TOPK = 256
CHK_HEADS = 16
CHK_LEFT_CHUNKS = 8
MAX_REL_DIST = 128

WIDTH = 512
T_DQ, T_DK, T_IQ, T_FQ, T_FK, T_FV, T_FG, T_DV, T_DG = range(9)
N_ROPE_TILES = 3
L_IK = 0
L_FL = 64
L_IW = 72
L_AUG = 64

VMEM_LIMIT = 52 * 1024 * 1024

INT_MIN = -2147483648
I16 = jnp.int16
I16_MIN = -32768
HALF_BITS = 16
HALF_MASK = 0xFFFF
PACKED_ROWS = 16


def _sortable_key_of(value):
    b = int(np.array(value, np.float32).view(np.int32))
    return b ^ ((b >> 31) & 0x7FFFFFFF)


KEY_OF_NEG_INF = _sortable_key_of(NEG_INF)


def _cparams(n_axes):
    return pltpu.CompilerParams(dimension_semantics=("arbitrary",) * n_axes,
                                vmem_limit_bytes=VMEM_LIMIT)


def _dot(a, b):
    return jnp.dot(a, b, preferred_element_type=F32)


def _block_start(index, size):
    return index * size if isinstance(index, int) else pl.multiple_of(index * size, size)


def _rope(t, cos_t, sin_t, lane):
    first = (lane & (HALF - 1)) < (ROT_DIM // 2)
    partner = jnp.where(first, pltpu.roll(t, LANES - ROT_DIM // 2, 1), pltpu.roll(t, ROT_DIM // 2, 1))
    return t * cos_t + partner * sin_t


def _split3(x):
    hi = x.astype(BF16)
    r1 = x - hi.astype(F32)
    mid = r1.astype(BF16)
    lo = (r1 - mid.astype(F32)).astype(BF16)
    return hi, mid, lo


def _proj0_kernel(x_ref, g_ref, w_ref, ws_ref, cos_ref, sin_ref, par_ref,
                  p_ref, ik_ref, sm_ref, h_scr, carry_scr, *, tm):
    s = pl.program_id(1)
    j = pl.program_id(2)
    lane = lax.broadcasted_iota(I32, (tm, LANES), 1)

    @pl.when(j == 0)
    def _prologue():
        xf = x_ref[0]
        ms = jnp.mean(xf * xf, axis=-1, keepdims=True)
        hb = ((xf * lax.rsqrt(ms + RMS_EPS)) * g_ref[...]).astype(BF16)
        h_scr[...] = hb
        small = jnp.dot(hb, ws_ref[...], preferred_element_type=F32)

        is_ik = lane < IDX_DIM
        mu = jnp.sum(jnp.where(is_ik, small, 0.0), axis=-1, keepdims=True) * (1.0 / IDX_DIM)
        xc = small - mu
        var = jnp.sum(jnp.where(is_ik, xc * xc, 0.0), axis=-1, keepdims=True) * (1.0 / IDX_DIM)
        y = xc * lax.rsqrt(var + LN_EPS) * par_ref[1:2, :] + par_ref[2:3, :]
        yr = _rope(y, cos_ref[0], sin_ref[0], lane)
        ik_ref[0] = jnp.where(is_ik, yr, pltpu.roll(yr, HALF, 1)).astype(BF16)

        z = small + par_ref[0:1, :]
        ls = jnp.minimum(z, 0.0) - jnp.log1p(jnp.exp(-jnp.abs(z)))
        is_f = (lane >= L_FL) & (lane < L_FL + FOX_HEADS)
        ls = jnp.where(is_f, ls, 0.0)
        is_iw = (lane >= L_IW) & (lane < L_IW + IDX_HEADS)
        other = small * jnp.where(is_iw, float((IDX_HEADS * IDX_DIM) ** -0.5), 1.0)

        @pl.when(s == 0)
        def _():
            carry_scr[...] = jnp.zeros_like(carry_scr)

        blk = LANES
        tri = (lax.broadcasted_iota(I32, (blk, blk), 0) >= lax.broadcasted_iota(I32, (blk, blk), 1))
        tri = jnp.where(tri, 1.0, 0.0).astype(BF16)
        carry = carry_scr[0:1, :]
        lane_blk = lax.broadcasted_iota(I32, (blk, LANES), 1)
        is_f_blk = (lane_blk >= L_FL) & (lane_blk < L_FL + FOX_HEADS)
        for r in range(tm // blk):
            hi, mid, lo = _split3(ls[r * blk:(r + 1) * blk])
            cum = (jnp.dot(tri, hi, preferred_element_type=F32)
                   + jnp.dot(tri, mid, preferred_element_type=F32)
                   + jnp.dot(tri, lo, preferred_element_type=F32)) + carry
            sm_ref[0, r * blk:(r + 1) * blk, :] = jnp.where(is_f_blk, cum, other[r * blk:(r + 1) * blk])
            carry = cum[blk - 1:blk, :]
        carry_scr[0:1, :] = carry

    res = jnp.dot(h_scr[...], w_ref[...], preferred_element_type=F32)
    res = res * jnp.where((j == T_DQ) | (j == T_FQ), LOG2E, 1.0)

    @pl.when(j < N_ROPE_TILES)
    def _with_rope():
        c = cos_ref[0]
        sn = sin_ref[0]
        for hp in range(WIDTH // LANES):
            p_ref[0, 0, hp] = _rope(res[:, hp * LANES:(hp + 1) * LANES], c, sn, lane).astype(BF16)

    @pl.when(j >= N_ROPE_TILES)
    def _plain():
        for hp in range(WIDTH // LANES):
            p_ref[0, 0, hp] = res[:, hp * LANES:(hp + 1) * LANES].astype(BF16)


def _proj0(x, g, w_main, w_small, cos_t, sin_t, par, *, tm):
    B, S, D = x.shape
    n_tiles = w_main.shape[1] // WIDTH
    hp = WIDTH // LANES
    return pl.pallas_call(
        functools.partial(_proj0_kernel, tm=tm),
        grid=(B, S // tm, n_tiles),
        in_specs=[
            pl.BlockSpec((1, tm, D), lambda b, s, j: (b, s, 0)),
            pl.BlockSpec((1, D), lambda b, s, j: (0, 0)),
            pl.BlockSpec((D, WIDTH), lambda b, s, j: (0, j)),
            pl.BlockSpec((D, LANES), lambda b, s, j: (0, 0)),
            pl.BlockSpec((1, tm, LANES), lambda b, s, j: (b, s, 0)),
            pl.BlockSpec((1, tm, LANES), lambda b, s, j: (b, s, 0)),
            pl.BlockSpec((8, LANES), lambda b, s, j: (0, 0)),
        ],
        out_specs=[
            pl.BlockSpec((1, 1, hp, tm, LANES), lambda b, s, j: (j, b, 0, s, 0)),
            pl.BlockSpec((1, tm, LANES), lambda b, s, j: (b, s, 0)),
            pl.BlockSpec((1, tm, LANES), lambda b, s, j: (b, s, 0)),
        ],
        out_shape=[
            jax.ShapeDtypeStruct((n_tiles, B, hp, S, LANES), BF16),
            jax.ShapeDtypeStruct((B, S, LANES), BF16),
            jax.ShapeDtypeStruct((B, S, LANES), F32),
        ],
        scratch_shapes=[pltpu.VMEM((tm, D), BF16), pltpu.VMEM((8, LANES), F32)],
        compiler_params=_cparams(3),
        name="proj0",
    )(x, g, w_main, w_small, cos_t, sin_t, par)


def _proj1_kernel(x_ref, g_ref, w_ref, p_ref, h_scr, *, q_tiles):
    j = pl.program_id(2)

    @pl.when(j == 0)
    def _prologue():
        xf = x_ref[0]
        ms = jnp.mean(xf * xf, axis=-1, keepdims=True)
        h_scr[...] = ((xf * lax.rsqrt(ms + RMS_EPS)) * g_ref[...]).astype(BF16)

    res = jnp.dot(h_scr[...], w_ref[...], preferred_element_type=F32)
    res = res * jnp.where(j < q_tiles, LOG2E, 1.0)
    for hp in range(WIDTH // LANES):
        p_ref[0, 0, hp] = res[:, hp * LANES:(hp + 1) * LANES].astype(BF16)


def _proj1(x, g, w, *, tm, n_tensors):
    B, S, D = x.shape
    n_tiles = w.shape[1] // WIDTH
    per = n_tiles // n_tensors
    hp = WIDTH // LANES
    return pl.pallas_call(
        functools.partial(_proj1_kernel, q_tiles=per),
        grid=(B, S // tm, n_tiles),
        in_specs=[
            pl.BlockSpec((1, tm, D), lambda b, s, j: (b, s, 0)),
            pl.BlockSpec((1, D), lambda b, s, j: (0, 0)),
            pl.BlockSpec((D, WIDTH), lambda b, s, j: (0, j)),
        ],
        out_specs=pl.BlockSpec((1, 1, hp, tm, LANES), lambda b, s, j: (j // per, b, j % per, s, 0)),
        out_shape=jax.ShapeDtypeStruct((n_tensors, B, per * hp, S, LANES), BF16),
        scratch_shapes=[pltpu.VMEM((tm, D), BF16)],
        compiler_params=_cparams(3),
        name="proj1",
    )(x, g, w)


VT_ROWS = HEAD_DIM + PACKED_ROWS


def _flash_scratch(n_heads, tq):
    return [pltpu.VMEM((n_heads, tq, tq), F32), pltpu.VMEM((n_heads, 1, tq), F32),
            pltpu.VMEM((n_heads, VT_ROWS, tq), F32)]


def _store_values_t(vt_scr, hp, c, v_blk):
    v_t = v_blk.astype(F32).T
    tk = v_t.shape[1]
    ones_row = jnp.where(lax.broadcasted_iota(I32, (PACKED_ROWS, tk), 0) == 0, 1.0, 0.0).astype(BF16)
    for hh in range(2):
        vt_scr[2 * hp + hh, c, 0:HEAD_DIM, :] = v_t[hh * HEAD_DIM:(hh + 1) * HEAD_DIM].astype(BF16)
        vt_scr[2 * hp + hh, c, HEAD_DIM:VT_ROWS, :] = ones_row


def _flash_update(h, s_t, vt, m_scr, acc_scr):
    m = m_scr[h]
    m_new = jnp.maximum(m, jnp.max(s_t, axis=0, keepdims=True))
    alpha = jnp.exp2(m - m_new)
    p = jnp.exp2((s_t - m_new).astype(BF16))
    m_scr[h] = m_new
    acc_scr[h] = alpha * acc_scr[h] + _dot(vt, p)


def _flash_attend(n_heads, lo, hi, scores_fn, values_fn, scratch, last_scores_fixup=None):
    s_scr, m_scr, acc_scr = scratch
    m_scr[...] = jnp.full(m_scr.shape, NEG_INF, F32)
    acc_scr[...] = jnp.zeros(acc_scr.shape, F32)
    for h in range(n_heads):
        s_scr[h] = scores_fn(lo, h)

    def body(kb, _):
        for h in range(n_heads):
            s_t = s_scr[h]
            s_scr[h] = scores_fn(kb + 1, h)
            _flash_update(h, s_t, values_fn(kb, h), m_scr, acc_scr)
        return 0
    lax.fori_loop(lo, hi - 1, body, 0)

    for h in range(n_heads):
        s_t = s_scr[h]
        if last_scores_fixup is not None:
            s_t = last_scores_fixup(s_t)
        _flash_update(h, s_t, values_fn(hi - 1, h), m_scr, acc_scr)


def _finish_pair(hp, gate, acc_scr):
    heads = [acc_scr[2 * hp + hh, 0:HEAD_DIM, :] / acc_scr[2 * hp + hh, HEAD_DIM:HEAD_DIM + 1, :] for hh in range(2)]
    g = gate.astype(F32)
    return (jnp.concatenate(heads, axis=0).T * (g * jax.nn.sigmoid(g))).astype(BF16)


def _split_pair(q, lane):
    qf = q.astype(F32)
    lo = lane < HALF
    return jnp.where(lo, qf, 0.0).T.astype(BF16), jnp.where(lo, 0.0, qf).T.astype(BF16)


def _fox_aug(base, f_col, lane, is_key):
    hi, mid, lo = (t.astype(F32) for t in _split3(f_col))
    if is_key:
        a, b, c, d = -hi, -mid, -lo, 1.0
        tail = jnp.where(lane < L_AUG + 6, d, 0.0)
        aug = jnp.where(lane == L_AUG, a, jnp.where(lane == L_AUG + 1, b, jnp.where(lane == L_AUG + 2, c, tail)))
    else:
        tail = jnp.where(lane == L_AUG + 3, hi, jnp.where(lane == L_AUG + 4, mid,
                                                         jnp.where(lane == L_AUG + 5, lo, 0.0)))
        aug = jnp.where(lane < L_AUG + 3, 1.0, tail)
    out = jnp.where(lane < L_AUG, base, aug)
    return out.astype(BF16) if is_key else out.T.astype(BF16)


def _fox_kernel(q_ref, k_ref, v_ref, g_ref, f_ref, o_ref, ka_scr, vt_scr, *flash_scr, tq):
    qi = pl.program_id(1)
    tk = tq
    n_pairs = k_ref.shape[2]
    n_blocks = k_ref.shape[3] // tk
    lane = lax.broadcasted_iota(I32, (tq, LANES), 1)

    def head_f(f_tile, h):
        return jnp.sum(jnp.where(lane == L_FL + h, f_tile, 0.0), axis=-1, keepdims=True) * LOG2E

    def head_base(t, hh):
        return t if hh == 0 else pltpu.roll(t, HALF, 1)

    @pl.when(qi == 0)
    def _prepare():
        def body(c, _):
            r0 = pl.multiple_of(c * tk, tk)
            ff = f_ref[0, pl.ds(r0, tk), :]
            for hp in range(n_pairs):
                kf = k_ref[0, 0, hp, pl.ds(r0, tk), :].astype(F32)
                for hh in range(2):
                    h = 2 * hp + hh
                    ka_scr[h, pl.ds(r0, tk), :] = _fox_aug(head_base(kf, hh), head_f(ff, h), lane, True)
                _store_values_t(vt_scr, hp, c, v_ref[0, 0, hp, pl.ds(r0, tk), :])
            return 0
        lax.fori_loop(0, n_blocks, body, 0)

    q0 = pl.multiple_of(qi * tq, tq)
    fq = f_ref[0, pl.ds(q0, tq), :]
    qa = []
    for hp in range(n_pairs):
        qf = q_ref[0, 0, hp].astype(F32)
        qa.extend(_fox_aug(head_base(qf, hh), head_f(fq, 2 * hp + hh), lane, False) for hh in range(2))

    def scores(kb, h):
        return _dot(ka_scr[h, pl.ds(_block_start(kb, tk), tk), :], qa[h])

    def values(kb, h):
        return vt_scr[h, kb]

    def causal(s_t):
        future = lax.broadcasted_iota(I32, (tk, tq), 0) > lax.broadcasted_iota(I32, (tk, tq), 1)
        return jnp.where(future, NEG_INF, s_t)

    _flash_attend(2 * n_pairs, 0, qi + 1, scores, values, flash_scr, last_scores_fixup=causal)
    for hp in range(n_pairs):
        o_ref[0, hp] = _finish_pair(hp, g_ref[0, 0, hp], flash_scr[2])


def _fox(p0, small, *, tq):
    _, B, HP, S, _ = p0.shape
    spec_q = lambda t: pl.BlockSpec((1, 1, HP, tq, LANES), lambda b, q, t=t: (t, b, 0, q, 0))
    spec_s = lambda t: pl.BlockSpec((1, 1, HP, S, LANES), lambda b, q, t=t: (t, b, 0, 0, 0),
                                    pipeline_mode=pl.Buffered(1))
    return pl.pallas_call(
        functools.partial(_fox_kernel, tq=tq),
        grid=(B, S // tq),
        in_specs=[spec_q(T_FQ), spec_s(T_FK), spec_s(T_FV), spec_q(T_FG),
                  pl.BlockSpec((1, S, LANES), lambda b, q: (b, 0, 0), pipeline_mode=pl.Buffered(1))],
        out_specs=pl.BlockSpec((1, HP, tq, LANES), lambda b, q: (b, 0, q, 0)),
        out_shape=jax.ShapeDtypeStruct((B, HP, S, LANES), BF16),
        scratch_shapes=[pltpu.VMEM((2 * HP, S, LANES), BF16), pltpu.VMEM((2 * HP, S // tq, VT_ROWS, tq), BF16)]
        + _flash_scratch(2 * HP, tq),
        compiler_params=_cparams(2),
        name="fox_attn",
    )(p0, p0, p0, p0, small)


def _dsa_kernel(q_ref, k_ref, v_ref, iq_ref, g_ref, ik_ref, sm_ref, o_ref, vt_scr, key_scr, hi_scr, lo_scr,
                *flash_scr, tq, top_k):
    qi = pl.program_id(1)
    tk = tq
    n_pairs = k_ref.shape[2]
    seq = k_ref.shape[3]
    n_blocks = seq // tk
    lane = lax.broadcasted_iota(I32, (tq, LANES), 1)

    @pl.when(qi == 0)
    def _prepare():
        def body(c, _):
            r0 = pl.multiple_of(c * tk, tk)
            for hp in range(n_pairs):
                _store_values_t(vt_scr, hp, c, v_ref[0, 0, hp, pl.ds(r0, tk), :])
            return 0
        lax.fori_loop(0, n_blocks, body, 0)

    iw_t = sm_ref[0].T[L_IW:L_IW + IDX_HEADS, :]
    iq_m = []
    for hp in range(IDX_HEADS // 2):
        iq_m.extend(_split_pair(iq_ref[0, 0, hp], lane))

    def index_keys(r0):
        ikb = ik_ref[0, pl.ds(r0, tk), :]
        score = jnp.zeros((tk, tq), F32)
        for h in range(IDX_HEADS):
            score = score + jnp.maximum(_dot(ikb, iq_m[h]), 0.0) * iw_t[h:h + 1, :]
        bits = pltpu.bitcast(score, I32)
        return bits ^ ((bits >> 31) & 0x7FFFFFFF)

    def store_keys(r0, key):
        key_scr[pl.ds(r0, tk), :] = key
        hi_scr[pl.ds(r0, tk), :] = (key >> HALF_BITS).astype(I16)
        lo_scr[pl.ds(r0, tk), :] = ((key & HALF_MASK) + I16_MIN).astype(I16)

    def fill(kb, _):
        r0 = pl.multiple_of(kb * tk, tk)
        store_keys(r0, index_keys(r0))
        return 0
    lax.fori_loop(0, qi, fill, 0)

    d0 = pl.multiple_of(qi * tk, tk)
    row = lax.broadcasted_iota(I32, (tk, tq), 0)
    col = lax.broadcasted_iota(I32, (tk, tq), 1)
    beyond_chunk = (row >> CHUNK_SHIFT) > (col >> CHUNK_SHIFT)
    store_keys(d0, jnp.where(beyond_chunk, INT_MIN, index_keys(d0)))
    pad0 = pl.multiple_of((qi + 1) * tk, tk)
    hi_scr[pl.ds(pad0, tk), :] = jnp.full((tk, tq), I16_MIN, I16)
    lo_scr[pl.ds(pad0, tk), :] = jnp.full((tk, tq), I16_MIN, I16)

    q_pos = qi * tq + lax.broadcasted_iota(I32, (1, tq), 1)
    n_beyond = (seq - ((q_pos >> CHUNK_SHIFT) + 1) * CHUNK).astype(F32)
    neg_hi = KEY_OF_NEG_INF >> HALF_BITS
    neg_lo = (KEY_OF_NEG_INF & HALF_MASK) + I16_MIN

    rows16 = 2 * tk // PACKED_ROWS

    def count16(ref, pred, t_s):
        t16 = t_s.astype(I16)

        def body(i, acc):
            r0 = pl.multiple_of(i * 2 * tk, 2 * tk)
            one = jnp.where(pred(ref[pl.ds(r0, 2 * tk), :], t16), jnp.int16(1), jnp.int16(0))
            for g in range(rows16):
                acc = acc + one[g * PACKED_ROWS:(g + 1) * PACKED_ROWS]
            return acc
        acc = lax.fori_loop(0, (qi + 2) // 2, body, jnp.zeros((PACKED_ROWS, tq), I16))
        return jnp.sum(acc.astype(F32), axis=0, keepdims=True)

    def search16(ref, need, beyond, beyond_key):
        def bit_step(i, t_u):
            cand_u = t_u | lax.shift_left(jnp.int32(1), HALF_BITS - 1 - i)
            cand_s = cand_u + I16_MIN
            c = count16(ref, lambda k, t: k >= t, cand_s) + jnp.where(cand_s <= beyond_key, beyond, 0.0)
            return jnp.where(c >= need, cand_u, t_u)
        return lax.fori_loop(0, HALF_BITS, bit_step, jnp.zeros((1, tq), I32)) + I16_MIN

    thr_hi = search16(hi_scr, float(top_k), n_beyond, neg_hi)
    n_above_hi = count16(hi_scr, lambda k, t: k > t, thr_hi) + jnp.where(thr_hi < neg_hi, n_beyond, 0.0)

    thr_hi16 = thr_hi.astype(I16)

    def keep_low(i, _):
        r0 = pl.multiple_of(i * 2 * tk, 2 * tk)
        sl = pl.ds(r0, 2 * tk)
        lo_scr[sl, :] = jnp.where(hi_scr[sl, :] == thr_hi16, lo_scr[sl, :], jnp.int16(I16_MIN))
        return 0
    lax.fori_loop(0, (qi + 2) // 2, keep_low, 0)

    beyond_lo = jnp.where(thr_hi == neg_hi, n_beyond, 0.0)
    thr_lo = search16(lo_scr, top_k - n_above_hi, beyond_lo, neg_lo)
    n_above = (n_above_hi + count16(lo_scr, lambda k, t: k > t, thr_lo)
               + jnp.where(thr_lo < neg_lo, beyond_lo, 0.0))
    thr = lax.shift_left(thr_hi, HALF_BITS) | (thr_lo - I16_MIN)
    n_ties_kept = top_k - n_above

    strict_lower = jnp.where(row > col, 1.0, 0.0).astype(BF16)

    def make_bias(kb, ties_before):
        r0 = pl.multiple_of(kb * tk, tk)
        kblk = key_scr[pl.ds(r0, tk), :]
        eq = kblk == thr
        eq_f = jnp.where(eq, 1.0, 0.0)
        rank = jnp.dot(strict_lower, eq_f.astype(BF16), preferred_element_type=F32) + ties_before
        bias = jnp.where(kblk > thr, 0.0, jnp.where(eq, jnp.where(rank < n_ties_kept, 0.0, NEG_INF), NEG_INF))
        key_scr[pl.ds(r0, tk), :] = pltpu.bitcast(bias, I32)
        return ties_before + jnp.sum(eq_f, axis=0, keepdims=True)
    lax.fori_loop(0, qi + 1, make_bias, jnp.zeros((1, tq), F32))

    qm = []
    for hp in range(n_pairs):
        qm.extend(_split_pair(q_ref[0, 0, hp], lane))

    def scores(kb, h):
        r0 = _block_start(kb, tk)
        bias = pltpu.bitcast(key_scr[pl.ds(r0, tk), :], F32)
        return _dot(k_ref[0, 0, h // 2, pl.ds(r0, tk), :], qm[h]) + bias

    def values(kb, h):
        return vt_scr[h, kb]

    _flash_attend(2 * n_pairs, 0, qi + 1, scores, values, flash_scr)
    for hp in range(n_pairs):
        o_ref[0, hp] = _finish_pair(hp, g_ref[0, 0, hp], flash_scr[2])


def _dsa(p0, ik2, small, *, tq, top_k):
    _, B, HP, S, _ = p0.shape
    spec_q = lambda t: pl.BlockSpec((1, 1, HP, tq, LANES), lambda b, q, t=t: (t, b, 0, q, 0))
    spec_s = lambda t: pl.BlockSpec((1, 1, HP, S, LANES), lambda b, q, t=t: (t, b, 0, 0, 0),
                                    pipeline_mode=pl.Buffered(1))
    return pl.pallas_call(
        functools.partial(_dsa_kernel, tq=tq, top_k=top_k),
        grid=(B, S // tq),
        in_specs=[spec_q(T_DQ), spec_s(T_DK), spec_s(T_DV), spec_q(T_IQ), spec_q(T_DG),
                  pl.BlockSpec((1, S, LANES), lambda b, q: (b, 0, 0), pipeline_mode=pl.Buffered(1)),
                  pl.BlockSpec((1, tq, LANES), lambda b, q: (b, q, 0))],
        out_specs=pl.BlockSpec((1, HP, tq, LANES), lambda b, q: (b, 0, q, 0)),
        out_shape=jax.ShapeDtypeStruct((B, HP, S, LANES), BF16),
        scratch_shapes=[pltpu.VMEM((2 * HP, S // tq, VT_ROWS, tq), BF16), pltpu.VMEM((S, tq), I32),
                        pltpu.VMEM((S + tq, tq), I16), pltpu.VMEM((S + tq, tq), I16)]
        + _flash_scratch(2 * HP, tq),
        compiler_params=_cparams(2),
        name="dsa_attn",
    )(p0, p0, p0, p0, p0, ik2, small)


def _chunk_kernel(q_ref, k_ref, v_ref, g_ref, bt_ref, o_ref, vt_scr, *flash_scr, tq):
    qi = pl.program_id(2)
    tk = tq
    n_pairs = k_ref.shape[2]
    n_blocks = k_ref.shape[3] // tk
    band_blocks = bt_ref.shape[1] // tk
    lane = lax.broadcasted_iota(I32, (tq, LANES), 1)

    @pl.when(qi == 0)
    def _prepare():
        def body(c, _):
            r0 = pl.multiple_of(c * tk, tk)
            for hp in range(n_pairs):
                _store_values_t(vt_scr, hp, c, v_ref[0, 0, hp, pl.ds(r0, tk), :])
            return 0
        lax.fori_loop(0, n_blocks, body, 0)

    qm = []
    for hp in range(n_pairs):
        qm.extend(_split_pair(q_ref[0, 0, hp], lane))

    def scores(jb, h):
        r0 = _block_start(qi - (band_blocks - 1) + jb, tk)
        b0 = _block_start(jb, tk)
        return _dot(k_ref[0, 0, h // 2, pl.ds(r0, tk), :], qm[h]) + bt_ref[h, pl.ds(b0, tk), :]

    def values(jb, h):
        return vt_scr[h, qi - (band_blocks - 1) + jb]

    first = jnp.maximum(band_blocks - 1 - qi, 0)
    _flash_attend(2 * n_pairs, first, band_blocks, scores, values, flash_scr)
    for hp in range(n_pairs):
        o_ref[0, hp] = _finish_pair(hp, g_ref[0, 0, hp], flash_scr[2])


def _chunk_attn(p1, bias_t, *, tq, pairs_per_step):
    _, B, HP, S, _ = p1.shape
    band = bias_t.shape[1]
    pp = pairs_per_step
    spec_q = lambda t: pl.BlockSpec((1, 1, pp, tq, LANES), lambda b, h, q, t=t: (t, b, h, q, 0))
    spec_s = lambda t: pl.BlockSpec((1, 1, pp, S, LANES), lambda b, h, q, t=t: (t, b, h, 0, 0),
                                    pipeline_mode=pl.Buffered(1))
    return pl.pallas_call(
        functools.partial(_chunk_kernel, tq=tq),
        grid=(B, HP // pp, S // tq),
        in_specs=[spec_q(0), spec_s(1), spec_s(2), spec_q(3),
                  pl.BlockSpec((2 * pp, band, tq), lambda b, h, q: (h, 0, 0), pipeline_mode=pl.Buffered(1))],
        out_specs=pl.BlockSpec((1, pp, tq, LANES), lambda b, h, q: (b, h, q, 0)),
        out_shape=jax.ShapeDtypeStruct((B, HP, S, LANES), BF16),
        scratch_shapes=[pltpu.VMEM((2 * pp, S // tq, VT_ROWS, tq), BF16)] + _flash_scratch(2 * pp, tq),
        compiler_params=_cparams(3),
        name="chunk_attn",
    )(p1, p1, p1, p1, bias_t)


def _band_bias_t(rel_table, *, tq):
    left = CHK_LEFT_CHUNKS * CHUNK
    pad = -(-left // tq) * tq
    rows = pad + tq
    r = jnp.arange(rows)[:, None]
    c = jnp.arange(tq)[None, :]
    back = (c + pad) // CHUNK - r // CHUNK
    in_band = (back >= 0) & (back <= CHK_LEFT_CHUNKS)
    period = rows + tq
    j = jnp.arange(period)
    c_minus_r = jnp.where(j < tq, j, j - period)
    bucket = jnp.clip(c_minus_r + pad, -MAX_REL_DIST, MAX_REL_DIST) + MAX_REL_DIST
    per_offset = rel_table.astype(F32)[:, bucket]
    n_heads = rel_table.shape[0]
    bias = jnp.tile(per_offset, (1, rows))[:, :rows * (period - 1)].reshape(n_heads, rows, period - 1)[:, :, :tq]
    return jnp.where(in_band[None], bias * LOG2E, NEG_INF)


def _out_kernel(*refs, n_in):
    a_refs = refs[:n_in]
    w_ref, x_ref, g_ref, o_ref = refs[n_in:]
    parts = []
    for a in a_refs:
        parts.extend(a[0, hp] for hp in range(a.shape[1]))
    att = jnp.concatenate(parts, axis=-1)
    y = jnp.dot(att, w_ref[...], preferred_element_type=F32)
    ms = jnp.mean(y * y, axis=-1, keepdims=True)
    o_ref[0] = x_ref[0] + (y * lax.rsqrt(ms + RMS_EPS)) * g_ref[...]


def _out_proj(atts, w, x, g, *, tm):
    B, S, D = x.shape
    n_in = len(atts)
    a_specs = [pl.BlockSpec((1, a.shape[1], tm, LANES), lambda b, s: (b, 0, s, 0)) for a in atts]
    return pl.pallas_call(
        functools.partial(_out_kernel, n_in=n_in),
        grid=(B, S // tm),
        in_specs=a_specs + [
            pl.BlockSpec(w.shape, lambda b, s: (0, 0)),
            pl.BlockSpec((1, tm, D), lambda b, s: (b, s, 0)),
            pl.BlockSpec((1, D), lambda b, s: (0, 0)),
        ],
        out_specs=pl.BlockSpec((1, tm, D), lambda b, s: (b, s, 0)),
        out_shape=jax.ShapeDtypeStruct((B, S, D), F32),
        compiler_params=_cparams(2),
        name=f"out_proj{n_in}",
    )(*atts, w, x, g)


def _rope_tiles(positions):
    inv_freq = ROPE_THETA ** (-jnp.arange(0, ROT_DIM, 2, dtype=F32) / ROT_DIM)
    ang = positions.astype(F32)[..., None] * inv_freq
    cos, sin = jnp.cos(ang), jnp.sin(ang)
    half = ROT_DIM // 2
    one = jnp.ones(ang.shape[:-1] + (HEAD_DIM - ROT_DIM,), F32)
    cos_head = jnp.concatenate([cos, cos, one], axis=-1)
    sin_head = jnp.concatenate([-sin, sin, 0.0 * one], axis=-1)
    assert cos_head.shape[-1] == HEAD_DIM and half * 2 == ROT_DIM
    return jnp.tile(cos_head, (1, 1, 2)), jnp.tile(sin_head, (1, 1, 2))


def _layer0_weights(w_in, b_forget, idx_k_g, idx_k_b):
    scale = HEAD_DIM ** -0.5
    off = 0
    cols = {}
    for name, n in (("fq", WIDTH), ("fk", WIDTH), ("fv", WIDTH), ("fl", FOX_HEADS), ("fg", WIDTH),
                    ("dq", WIDTH), ("dk", WIDTH), ("dv", WIDTH), ("iq", WIDTH), ("ik", IDX_DIM),
                    ("iw", IDX_HEADS), ("dg", WIDTH)):
        cols[name] = w_in[:, off:off + n]
        off += n
    assert off == w_in.shape[1]
    order = [None] * 9
    order[T_DQ], order[T_DK], order[T_IQ] = cols["dq"] * scale, cols["dk"], cols["iq"]
    order[T_FQ], order[T_FK], order[T_FV], order[T_FG] = cols["fq"] * scale, cols["fk"], cols["fv"], cols["fg"]
    order[T_DV], order[T_DG] = cols["dv"], cols["dg"]
    w_main = jnp.concatenate(order, axis=1).astype(BF16)
    pad = jnp.zeros((w_in.shape[0], LANES - L_IW - IDX_HEADS), w_in.dtype)
    w_small = jnp.concatenate([cols["ik"], cols["fl"], cols["iw"], pad], axis=1).astype(BF16)
    par = jnp.zeros((8, LANES), F32)
    par = par.at[0, L_FL:L_FL + FOX_HEADS].set(b_forget.astype(F32))
    par = par.at[1, :IDX_DIM].set(idx_k_g.astype(F32))
    par = par.at[2, :IDX_DIM].set(idx_k_b.astype(F32))
    return w_main, w_small, par


def _even_layer(x, cos_t, sin_t, pre_g, post_g, w_in, b_forget, idx_k_g, idx_k_b, w_out, top_k, cfg):
    w_main, w_small, par = _layer0_weights(w_in, b_forget, idx_k_g, idx_k_b)
    p0, ik2, small = _proj0(x, pre_g[None].astype(F32), w_main, w_small, cos_t, sin_t, par, tm=cfg["tm_proj"])
    fox = _fox(p0, small, tq=cfg["tq"])
    dsa = _dsa(p0, ik2, small, tq=cfg["tq"], top_k=top_k)
    return _out_proj([fox, dsa], w_out.astype(BF16), x, post_g[None].astype(F32), tm=cfg["tm_out"])


def _odd_layer(x, pre_g, post_g, w_in, rel_table, w_out, cfg):
    scale = HEAD_DIM ** -0.5
    width = CHK_HEADS * HEAD_DIM
    w = jnp.concatenate([w_in[:, :width] * scale, w_in[:, width:]], axis=1).astype(BF16)
    p1 = _proj1(x, pre_g[None].astype(F32), w, tm=cfg["tm_proj"], n_tensors=4)
    att = _chunk_attn(p1, _band_bias_t(rel_table, tq=cfg["tq"]), tq=cfg["tq"], pairs_per_step=4)
    return _out_proj([att], w_out.astype(BF16), x, post_g[None].astype(F32), tm=cfg["tm_out"])


def _config(seq):
    tq = 256
    assert seq % tq == 0 and tq % CHUNK == 0
    tm_proj = min(1024, seq)
    tm_out = min(512, seq)
    assert seq % tm_proj == 0 and seq % tm_out == 0
    return {"tq": tq, "tm_proj": tm_proj, "tm_out": tm_out}


def kernel(x, positions, pre_norm_g, post_norm_g, w_in_even, b_forget, idx_k_g, idx_k_b,
           w_out_even, w_in_odd, rel_bias, w_out_odd):
    seq = x.shape[1]
    depth = pre_norm_g.shape[0]
    cfg = _config(seq)
    top_k = min(DSA_TOPK, seq // 4)
    cos_t, sin_t = _rope_tiles(positions)
    for layer in range(depth):
        i = layer // 2
        if layer % 2 == 0:
            x = _even_layer(x, cos_t, sin_t, pre_norm_g[layer], post_norm_g[layer], w_in_even[i],
                            b_forget[i], idx_k_g[i], idx_k_b[i], w_out_even[i], top_k, cfg)
        else:
            x = _odd_layer(x, pre_norm_g[layer], post_norm_g[layer], w_in_odd[i], rel_bias[i], w_out_odd[i], cfg)
    return x
```

```python
import functools

import numpy as np
import jax
import jax.numpy as jnp
from jax import lax
from jax.experimental import pallas as pl
from jax.experimental.pallas import tpu as pltpu

F32 = jnp.float32
BF16 = jnp.bfloat16
I32 = jnp.int32

LANES = 128
MXU_COLS = 256
HEAD_DIM = 64
HALF = HEAD_DIM
ROT_DIM = HEAD_DIM // 4
ROPE_THETA = 500000.0
RMS_EPS = 1e-6
LN_EPS = 1e-6
NEG_INF = -1e30
LOG2E = 1.4426950408889634
CHUNK = 64
CHUNK_SHIFT = 6
FOX_HEADS = 8
DSA_HEADS = 8
IDX_HEADS = 8
IDX_DIM = 64
DSA_TOPK = 256
CHK_HEADS = 16
CHK_LEFT_CHUNKS = 8
MAX_REL_DIST = 128

WIDTH = 512
T_DQ, T_DK, T_IQ, T_FQ, T_FK, T_FV, T_FG, T_DV, T_DG = range(9)
N_ROPE_TILES = 3
L_IK = 0
L_FL = 64
L_IW = 72
L_AUG = 64

VMEM_LIMIT = 52 * 1024 * 1024

INT_MIN = -2147483648
I16 = jnp.int16
I16_MIN = -32768
HALF_BITS = 16
HALF_MASK = 0xFFFF
PACKED_ROWS = 16
COUNT_CHAINS = 4


def _sortable_key_of(value):
    b = int(np.array(value, np.float32).view(np.int32))
    return b ^ ((b >> 31) & 0x7FFFFFFF)


KEY_OF_NEG_INF = _sortable_key_of(NEG_INF)


def _cparams(n_axes):
    return pltpu.CompilerParams(dimension_semantics=("arbitrary",) * n_axes,
                                vmem_limit_bytes=VMEM_LIMIT)


def _dot(a, b):
    return jnp.dot(a, b, preferred_element_type=F32)


def _block_start(index, size):
    return index * size if isinstance(index, int) else pl.multiple_of(index * size, size)


def _rope(t, cos_t, sin_t, lane):
    first = (lane & (HALF - 1)) < (ROT_DIM // 2)
    partner = jnp.where(first, pltpu.roll(t, LANES - ROT_DIM // 2, 1), pltpu.roll(t, ROT_DIM // 2, 1))
    return t * cos_t + partner * sin_t


def _split3(x):
    hi = x.astype(BF16)
    r1 = x - hi.astype(F32)
    mid = r1.astype(BF16)
    lo = (r1 - mid.astype(F32)).astype(BF16)
    return hi, mid, lo


def _project_tile(h_scr, w_ref, p_ref, scale, post):
    h = h_scr[...]
    half = MXU_COLS
    res = [_dot(h, w_ref[:, c * half:(c + 1) * half]) for c in range(WIDTH // half)]
    per = half // LANES
    for hp in range(WIDTH // LANES):
        t = res[hp // per][:, (hp % per) * LANES:(hp % per + 1) * LANES] * scale
        p_ref[0, 0, hp] = post(t).astype(BF16)


def _proj0_kernel(x_ref, g_ref, w_ref, ws_ref, cos_ref, sin_ref, par_ref,
                  p_ref, ik_ref, sm_ref, h_scr, carry_scr, *, tm):
    s = pl.program_id(1)
    j = pl.program_id(2)
    lane = lax.broadcasted_iota(I32, (tm, LANES), 1)

    @pl.when(j == 0)
    def _prologue():
        xf = x_ref[0]
        ms = jnp.mean(xf * xf, axis=-1, keepdims=True)
        hb = ((xf * lax.rsqrt(ms + RMS_EPS)) * g_ref[...]).astype(BF16)
        h_scr[...] = hb
        small = jnp.dot(hb, ws_ref[...], preferred_element_type=F32)

        is_ik = lane < IDX_DIM
        mu = jnp.sum(jnp.where(is_ik, small, 0.0), axis=-1, keepdims=True) * (1.0 / IDX_DIM)
        xc = small - mu
        var = jnp.sum(jnp.where(is_ik, xc * xc, 0.0), axis=-1, keepdims=True) * (1.0 / IDX_DIM)
        y = xc * lax.rsqrt(var + LN_EPS) * par_ref[1:2, :] + par_ref[2:3, :]
        yr = _rope(y, cos_ref[0], sin_ref[0], lane)
        ik_ref[0] = jnp.where(is_ik, yr, pltpu.roll(yr, HALF, 1)).astype(BF16)

        z = small + par_ref[0:1, :]
        ls = jnp.minimum(z, 0.0) - jnp.log1p(jnp.exp(-jnp.abs(z)))
        is_f = (lane >= L_FL) & (lane < L_FL + FOX_HEADS)
        ls = jnp.where(is_f, ls, 0.0)
        is_iw = (lane >= L_IW) & (lane < L_IW + IDX_HEADS)
        other = small * jnp.where(is_iw, float((IDX_HEADS * IDX_DIM) ** -0.5), 1.0)

        @pl.when(s == 0)
        def _():
            carry_scr[...] = jnp.zeros_like(carry_scr)

        blk = LANES
        tri = (lax.broadcasted_iota(I32, (blk, blk), 0) >= lax.broadcasted_iota(I32, (blk, blk), 1))
        tri = jnp.where(tri, 1.0, 0.0).astype(BF16)
        lane_blk = lax.broadcasted_iota(I32, (blk, LANES), 1)
        is_f_blk = (lane_blk >= L_FL) & (lane_blk < L_FL + FOX_HEADS)
        cums = []
        for r in range(tm // blk):
            hi, mid, lo = _split3(ls[r * blk:(r + 1) * blk])
            cums.append(_dot(tri, hi) + _dot(tri, mid) + _dot(tri, lo))
        carry = carry_scr[0:1, :]
        for r in range(tm // blk):
            sm_ref[0, r * blk:(r + 1) * blk, :] = jnp.where(is_f_blk, cums[r] + carry, other[r * blk:(r + 1) * blk])
            carry = carry + cums[r][blk - 1:blk, :]
        carry_scr[0:1, :] = carry

    scale = jnp.where((j == T_DQ) | (j == T_FQ), LOG2E, 1.0)

    @pl.when(j < N_ROPE_TILES)
    def _with_rope():
        c = cos_ref[0]
        sn = sin_ref[0]
        _project_tile(h_scr, w_ref, p_ref, scale, lambda t: _rope(t, c, sn, lane))

    @pl.when(j >= N_ROPE_TILES)
    def _plain():
        _project_tile(h_scr, w_ref, p_ref, scale, lambda t: t)


def _proj0(x, g, w_main, w_small, cos_t, sin_t, par, *, tm):
    B, S, D = x.shape
    n_tiles = w_main.shape[1] // WIDTH
    hp = WIDTH // LANES
    return pl.pallas_call(
        functools.partial(_proj0_kernel, tm=tm),
        grid=(B, S // tm, n_tiles),
        in_specs=[
            pl.BlockSpec((1, tm, D), lambda b, s, j: (b, s, 0)),
            pl.BlockSpec((1, D), lambda b, s, j: (0, 0)),
            pl.BlockSpec((D, WIDTH), lambda b, s, j: (0, j)),
            pl.BlockSpec((D, LANES), lambda b, s, j: (0, 0)),
            pl.BlockSpec((1, tm, LANES), lambda b, s, j: (b, s, 0)),
            pl.BlockSpec((1, tm, LANES), lambda b, s, j: (b, s, 0)),
            pl.BlockSpec((8, LANES), lambda b, s, j: (0, 0)),
        ],
        out_specs=[
            pl.BlockSpec((1, 1, hp, tm, LANES), lambda b, s, j: (j, b, 0, s, 0)),
            pl.BlockSpec((1, tm, LANES), lambda b, s, j: (b, s, 0)),
            pl.BlockSpec((1, tm, LANES), lambda b, s, j: (b, s, 0)),
        ],
        out_shape=[
            jax.ShapeDtypeStruct((n_tiles, B, hp, S, LANES), BF16),
            jax.ShapeDtypeStruct((B, S, LANES), BF16),
            jax.ShapeDtypeStruct((B, S, LANES), F32),
        ],
        scratch_shapes=[pltpu.VMEM((tm, D), BF16), pltpu.VMEM((8, LANES), F32)],
        compiler_params=_cparams(3),
        name="proj0",
    )(x, g, w_main, w_small, cos_t, sin_t, par)


def _proj1_kernel(x_ref, g_ref, w_ref, p_ref, h_scr, *, q_tiles):
    j = pl.program_id(2)

    @pl.when(j == 0)
    def _prologue():
        xf = x_ref[0]
        ms = jnp.mean(xf * xf, axis=-1, keepdims=True)
        h_scr[...] = ((xf * lax.rsqrt(ms + RMS_EPS)) * g_ref[...]).astype(BF16)

    scale = jnp.where(j < q_tiles, LOG2E, 1.0)
    _project_tile(h_scr, w_ref, p_ref, scale, lambda t: t)


def _proj1(x, g, w, *, tm, n_tensors):
    B, S, D = x.shape
    n_tiles = w.shape[1] // WIDTH
    per = n_tiles // n_tensors
    hp = WIDTH // LANES
    return pl.pallas_call(
        functools.partial(_proj1_kernel, q_tiles=per),
        grid=(B, S // tm, n_tiles),
        in_specs=[
            pl.BlockSpec((1, tm, D), lambda b, s, j: (b, s, 0)),
            pl.BlockSpec((1, D), lambda b, s, j: (0, 0)),
            pl.BlockSpec((D, WIDTH), lambda b, s, j: (0, j)),
        ],
        out_specs=pl.BlockSpec((1, 1, hp, tm, LANES), lambda b, s, j: (j // per, b, j % per, s, 0)),
        out_shape=jax.ShapeDtypeStruct((n_tensors, B, per * hp, S, LANES), BF16),
        scratch_shapes=[pltpu.VMEM((tm, D), BF16)],
        compiler_params=_cparams(3),
        name="proj1",
    )(x, g, w)


VT_ROWS = HEAD_DIM + PACKED_ROWS


def _flash_scratch(n_heads, tq):
    return [pltpu.VMEM((n_heads, tq, tq), F32), pltpu.VMEM((n_heads, 1, tq), F32),
            pltpu.VMEM((n_heads, VT_ROWS, tq), F32)]


def _store_values_t(vt_scr, hp, c, v_blk):
    v_t = v_blk.astype(F32).T
    tk = v_t.shape[1]
    ones_row = jnp.where(lax.broadcasted_iota(I32, (PACKED_ROWS, tk), 0) == 0, 1.0, 0.0).astype(BF16)
    for hh in range(2):
        vt_scr[2 * hp + hh, c, 0:HEAD_DIM, :] = v_t[hh * HEAD_DIM:(hh + 1) * HEAD_DIM].astype(BF16)
        vt_scr[2 * hp + hh, c, HEAD_DIM:VT_ROWS, :] = ones_row


def _flash_update(h, s_t, vt, m_scr, acc_scr):
    m = m_scr[h]
    m_new = jnp.maximum(m, jnp.max(s_t, axis=0, keepdims=True))
    alpha = jnp.exp2(m - m_new)
    p = jnp.exp2((s_t - m_new).astype(BF16))
    m_scr[h] = m_new
    acc_scr[h] = alpha * acc_scr[h] + _dot(vt, p)


def _flash_attend(n_heads, lo, hi, scores_fn, values_fn, scratch, last_scores_fixup=None):
    s_scr, m_scr, acc_scr = scratch
    m_scr[...] = jnp.full(m_scr.shape, NEG_INF, F32)
    acc_scr[...] = jnp.zeros(acc_scr.shape, F32)
    for h in range(n_heads):
        s_scr[h] = scores_fn(lo, h)

    def body(kb, _):
        for h in range(n_heads):
            s_t = s_scr[h]
            s_scr[h] = scores_fn(kb + 1, h)
            _flash_update(h, s_t, values_fn(kb, h), m_scr, acc_scr)
        return 0
    lax.fori_loop(lo, hi - 1, body, 0)

    for h in range(n_heads):
        s_t = s_scr[h]
        if last_scores_fixup is not None:
            s_t = last_scores_fixup(s_t)
        _flash_update(h, s_t, values_fn(hi - 1, h), m_scr, acc_scr)


def _finish_pair(hp, gate, acc_scr):
    heads = [acc_scr[2 * hp + hh, 0:HEAD_DIM, :] / acc_scr[2 * hp + hh, HEAD_DIM:HEAD_DIM + 1, :] for hh in range(2)]
    g = gate.astype(F32)
    return (jnp.concatenate(heads, axis=0).T * (g * jax.nn.sigmoid(g))).astype(BF16)


def _split_pair(q, lane):
    qf = q.astype(F32)
    lo = lane < HALF
    return jnp.where(lo, qf, 0.0).T.astype(BF16), jnp.where(lo, 0.0, qf).T.astype(BF16)


def _fox_aug(base, f_col, lane, is_key):
    hi, mid, lo = (t.astype(F32) for t in _split3(f_col))
    if is_key:
        a, b, c, d = -hi, -mid, -lo, 1.0
        tail = jnp.where(lane < L_AUG + 6, d, 0.0)
        aug = jnp.where(lane == L_AUG, a, jnp.where(lane == L_AUG + 1, b, jnp.where(lane == L_AUG + 2, c, tail)))
    else:
        tail = jnp.where(lane == L_AUG + 3, hi, jnp.where(lane == L_AUG + 4, mid,
                                                         jnp.where(lane == L_AUG + 5, lo, 0.0)))
        aug = jnp.where(lane < L_AUG + 3, 1.0, tail)
    out = jnp.where(lane < L_AUG, base, aug)
    return out.astype(BF16) if is_key else out.T.astype(BF16)


def _fox_kernel(q_ref, k_ref, v_ref, g_ref, f_ref, o_ref, ka_scr, vt_scr, *flash_scr, tq):
    qi = pl.program_id(1)
    tk = tq
    n_pairs = k_ref.shape[2]
    n_blocks = k_ref.shape[3] // tk
    lane = lax.broadcasted_iota(I32, (tq, LANES), 1)

    def head_f(f_tile, h):
        return jnp.sum(jnp.where(lane == L_FL + h, f_tile, 0.0), axis=-1, keepdims=True) * LOG2E

    def head_base(t, hh):
        return t if hh == 0 else pltpu.roll(t, HALF, 1)

    @pl.when(qi == 0)
    def _prepare():
        def body(c, _):
            r0 = pl.multiple_of(c * tk, tk)
            ff = f_ref[0, pl.ds(r0, tk), :]
            for hp in range(n_pairs):
                kf = k_ref[0, 0, hp, pl.ds(r0, tk), :].astype(F32)
                for hh in range(2):
                    h = 2 * hp + hh
                    ka_scr[h, pl.ds(r0, tk), :] = _fox_aug(head_base(kf, hh), head_f(ff, h), lane, True)
                _store_values_t(vt_scr, hp, c, v_ref[0, 0, hp, pl.ds(r0, tk), :])
            return 0
        lax.fori_loop(0, n_blocks, body, 0)

    q0 = pl.multiple_of(qi * tq, tq)
    fq = f_ref[0, pl.ds(q0, tq), :]
    qa = []
    for hp in range(n_pairs):
        qf = q_ref[0, 0, hp].astype(F32)
        qa.extend(_fox_aug(head_base(qf, hh), head_f(fq, 2 * hp + hh), lane, False) for hh in range(2))

    def scores(kb, h):
        return _dot(ka_scr[h, pl.ds(_block_start(kb, tk), tk), :], qa[h])

    def values(kb, h):
        return vt_scr[h, kb]

    def causal(s_t):
        future = lax.broadcasted_iota(I32, (tk, tq), 0) > lax.broadcasted_iota(I32, (tk, tq), 1)
        return jnp.where(future, NEG_INF, s_t)

    _flash_attend(2 * n_pairs, 0, qi + 1, scores, values, flash_scr, last_scores_fixup=causal)
    for hp in range(n_pairs):
        o_ref[0, hp] = _finish_pair(hp, g_ref[0, 0, hp], flash_scr[2])


def _fox(p0, small, *, tq):
    _, B, HP, S, _ = p0.shape
    spec_q = lambda t: pl.BlockSpec((1, 1, HP, tq, LANES), lambda b, q, t=t: (t, b, 0, q, 0))
    spec_s = lambda t: pl.BlockSpec((1, 1, HP, S, LANES), lambda b, q, t=t: (t, b, 0, 0, 0),
                                    pipeline_mode=pl.Buffered(1))
    return pl.pallas_call(
        functools.partial(_fox_kernel, tq=tq),
        grid=(B, S // tq),
        in_specs=[spec_q(T_FQ), spec_s(T_FK), spec_s(T_FV), spec_q(T_FG),
                  pl.BlockSpec((1, S, LANES), lambda b, q: (b, 0, 0), pipeline_mode=pl.Buffered(1))],
        out_specs=pl.BlockSpec((1, HP, tq, LANES), lambda b, q: (b, 0, q, 0)),
        out_shape=jax.ShapeDtypeStruct((B, HP, S, LANES), BF16),
        scratch_shapes=[pltpu.VMEM((2 * HP, S, LANES), BF16), pltpu.VMEM((2 * HP, S // tq, VT_ROWS, tq), BF16)]
        + _flash_scratch(2 * HP, tq),
        compiler_params=_cparams(2),
        name="fox_attn",
    )(p0, p0, p0, p0, small)


def _dsa_kernel(q_ref, k_ref, v_ref, iq_ref, g_ref, ik_ref, sm_ref, o_ref, vt_scr, key_scr, hi_scr, lo_scr,
                *flash_scr, tq, top_k):
    qi = pl.program_id(1)
    tk = tq
    n_pairs = k_ref.shape[2]
    seq = k_ref.shape[3]
    n_blocks = seq // tk
    lane = lax.broadcasted_iota(I32, (tq, LANES), 1)

    @pl.when(qi == 0)
    def _prepare():
        def body(c, _):
            r0 = pl.multiple_of(c * tk, tk)
            for hp in range(n_pairs):
                _store_values_t(vt_scr, hp, c, v_ref[0, 0, hp, pl.ds(r0, tk), :])
            return 0
        lax.fori_loop(0, n_blocks, body, 0)

    iw_t = sm_ref[0].T[L_IW:L_IW + IDX_HEADS, :]
    iq_m = []
    for hp in range(IDX_HEADS // 2):
        iq_m.extend(_split_pair(iq_ref[0, 0, hp], lane))

    def index_keys(r0):
        ikb = ik_ref[0, pl.ds(r0, tk), :]
        score = jnp.zeros((tk, tq), F32)
        for h in range(IDX_HEADS):
            score = score + jnp.maximum(_dot(ikb, iq_m[h]), 0.0) * iw_t[h:h + 1, :]
        bits = pltpu.bitcast(score, I32)
        return bits ^ ((bits >> 31) & 0x7FFFFFFF)

    def store_keys(r0, key):
        key_scr[pl.ds(r0, tk), :] = key
        hi_scr[pl.ds(r0, tk), :] = (key >> HALF_BITS).astype(I16)
        lo_scr[pl.ds(r0, tk), :] = ((key & HALF_MASK) + I16_MIN).astype(I16)

    def fill(kb, _):
        r0 = pl.multiple_of(kb * tk, tk)
        store_keys(r0, index_keys(r0))
        return 0
    lax.fori_loop(0, qi, fill, 0)

    d0 = pl.multiple_of(qi * tk, tk)
    row = lax.broadcasted_iota(I32, (tk, tq), 0)
    col = lax.broadcasted_iota(I32, (tk, tq), 1)
    beyond_chunk = (row >> CHUNK_SHIFT) > (col >> CHUNK_SHIFT)
    store_keys(d0, jnp.where(beyond_chunk, INT_MIN, index_keys(d0)))
    pad0 = pl.multiple_of((qi + 1) * tk, tk)
    hi_scr[pl.ds(pad0, tk), :] = jnp.full((tk, tq), I16_MIN, I16)
    lo_scr[pl.ds(pad0, tk), :] = jnp.full((tk, tq), I16_MIN, I16)

    q_pos = qi * tq + lax.broadcasted_iota(I32, (1, tq), 1)
    n_beyond = (seq - ((q_pos >> CHUNK_SHIFT) + 1) * CHUNK).astype(F32)
    neg_hi = KEY_OF_NEG_INF >> HALF_BITS
    neg_lo = (KEY_OF_NEG_INF & HALF_MASK) + I16_MIN

    rows16 = 2 * tk // PACKED_ROWS

    def count16(ref, pred, t_s):
        t16 = t_s.astype(I16)

        def body(i, accs):
            r0 = pl.multiple_of(i * 2 * tk, 2 * tk)
            one = jnp.where(pred(ref[pl.ds(r0, 2 * tk), :], t16), jnp.int16(1), jnp.int16(0))
            accs = list(accs)
            for g in range(rows16):
                accs[g % len(accs)] = accs[g % len(accs)] + one[g * PACKED_ROWS:(g + 1) * PACKED_ROWS]
            return tuple(accs)
        zero = jnp.zeros((PACKED_ROWS, tq), I16)
        accs = lax.fori_loop(0, (qi + 2) // 2, body, (zero,) * COUNT_CHAINS)
        return jnp.sum(sum(accs[1:], accs[0]).astype(F32), axis=0, keepdims=True)

    def search16(ref, need, beyond, beyond_key):
        def bit_step(i, t_u):
            cand_u = t_u | lax.shift_left(jnp.int32(1), HALF_BITS - 1 - i)
            cand_s = cand_u + I16_MIN
            c = count16(ref, lambda k, t: k >= t, cand_s) + jnp.where(cand_s <= beyond_key, beyond, 0.0)
            return jnp.where(c >= need, cand_u, t_u)
        return lax.fori_loop(0, HALF_BITS, bit_step, jnp.zeros((1, tq), I32)) + I16_MIN

    thr_hi = search16(hi_scr, float(top_k), n_beyond, neg_hi)
    n_above_hi = count16(hi_scr, lambda k, t: k > t, thr_hi) + jnp.where(thr_hi < neg_hi, n_beyond, 0.0)

    thr_hi16 = thr_hi.astype(I16)

    def keep_low(i, _):
        r0 = pl.multiple_of(i * 2 * tk, 2 * tk)
        sl = pl.ds(r0, 2 * tk)
        lo_scr[sl, :] = jnp.where(hi_scr[sl, :] == thr_hi16, lo_scr[sl, :], jnp.int16(I16_MIN))
        return 0
    lax.fori_loop(0, (qi + 2) // 2, keep_low, 0)

    beyond_lo = jnp.where(thr_hi == neg_hi, n_beyond, 0.0)
    thr_lo = search16(lo_scr, top_k - n_above_hi, beyond_lo, neg_lo)
    n_above_lo = count16(lo_scr, lambda k, t: k > t, thr_lo)
    n_ties = count16(lo_scr, lambda k, t: k >= t, thr_lo) - n_above_lo
    n_above = n_above_hi + n_above_lo + jnp.where(thr_lo < neg_lo, beyond_lo, 0.0)
    thr = lax.shift_left(thr_hi, HALF_BITS) | (thr_lo - I16_MIN)
    n_ties_kept = top_k - n_above
    must_rank_ties = jnp.max(jnp.where(n_ties > n_ties_kept, 1.0, 0.0)) > 0.5

    @pl.when(must_rank_ties)
    def _mask_with_tie_ranks():
        strict_lower = jnp.where(row > col, 1.0, 0.0).astype(BF16)

        def make_bias(kb, ties_before):
            r0 = pl.multiple_of(kb * tk, tk)
            kblk = key_scr[pl.ds(r0, tk), :]
            eq = kblk == thr
            eq_f = jnp.where(eq, 1.0, 0.0)
            rank = jnp.dot(strict_lower, eq_f.astype(BF16), preferred_element_type=F32) + ties_before
            bias = jnp.where(kblk > thr, 0.0, jnp.where(eq, jnp.where(rank < n_ties_kept, 0.0, NEG_INF), NEG_INF))
            key_scr[pl.ds(r0, tk), :] = pltpu.bitcast(bias, I32)
            return ties_before + jnp.sum(eq_f, axis=0, keepdims=True)
        lax.fori_loop(0, qi + 1, make_bias, jnp.zeros((1, tq), F32))

    @pl.when(jnp.logical_not(must_rank_ties))
    def _mask_all_ties_kept():
        def make_bias(kb, _):
            r0 = pl.multiple_of(kb * tk, tk)
            bias = jnp.where(key_scr[pl.ds(r0, tk), :] >= thr, 0.0, NEG_INF)
            key_scr[pl.ds(r0, tk), :] = pltpu.bitcast(bias, I32)
            return 0
        lax.fori_loop(0, qi + 1, make_bias, 0)

    qm = []
    for hp in range(n_pairs):
        qm.extend(_split_pair(q_ref[0, 0, hp], lane))

    def scores(kb, h):
        r0 = _block_start(kb, tk)
        bias = pltpu.bitcast(key_scr[pl.ds(r0, tk), :], F32)
        return _dot(k_ref[0, 0, h // 2, pl.ds(r0, tk), :], qm[h]) + bias

    def values(kb, h):
        return vt_scr[h, kb]

    _flash_attend(2 * n_pairs, 0, qi + 1, scores, values, flash_scr)
    for hp in range(n_pairs):
        o_ref[0, hp] = _finish_pair(hp, g_ref[0, 0, hp], flash_scr[2])


def _dsa(p0, ik2, small, *, tq, top_k):
    _, B, HP, S, _ = p0.shape
    spec_q = lambda t: pl.BlockSpec((1, 1, HP, tq, LANES), lambda b, q, t=t: (t, b, 0, q, 0))
    spec_s = lambda t: pl.BlockSpec((1, 1, HP, S, LANES), lambda b, q, t=t: (t, b, 0, 0, 0),
                                    pipeline_mode=pl.Buffered(1))
    return pl.pallas_call(
        functools.partial(_dsa_kernel, tq=tq, top_k=top_k),
        grid=(B, S // tq),
        in_specs=[spec_q(T_DQ), spec_s(T_DK), spec_s(T_DV), spec_q(T_IQ), spec_q(T_DG),
                  pl.BlockSpec((1, S, LANES), lambda b, q: (b, 0, 0), pipeline_mode=pl.Buffered(1)),
                  pl.BlockSpec((1, tq, LANES), lambda b, q: (b, q, 0))],
        out_specs=pl.BlockSpec((1, HP, tq, LANES), lambda b, q: (b, 0, q, 0)),
        out_shape=jax.ShapeDtypeStruct((B, HP, S, LANES), BF16),
        scratch_shapes=[pltpu.VMEM((2 * HP, S // tq, VT_ROWS, tq), BF16), pltpu.VMEM((S, tq), I32),
                        pltpu.VMEM((S + tq, tq), I16), pltpu.VMEM((S + tq, tq), I16)]
        + _flash_scratch(2 * HP, tq),
        compiler_params=_cparams(2),
        name="dsa_attn",
    )(p0, p0, p0, p0, p0, ik2, small)


def _chunk_kernel(q_ref, k_ref, v_ref, g_ref, bt_ref, o_ref, vt_scr, *flash_scr, tq):
    qi = pl.program_id(2)
    tk = tq
    n_pairs = k_ref.shape[2]
    n_blocks = k_ref.shape[3] // tk
    band_blocks = bt_ref.shape[1] // tk
    lane = lax.broadcasted_iota(I32, (tq, LANES), 1)

    @pl.when(qi == 0)
    def _prepare():
        def body(c, _):
            r0 = pl.multiple_of(c * tk, tk)
            for hp in range(n_pairs):
                _store_values_t(vt_scr, hp, c, v_ref[0, 0, hp, pl.ds(r0, tk), :])
            return 0
        lax.fori_loop(0, n_blocks, body, 0)

    qm = []
    for hp in range(n_pairs):
        qm.extend(_split_pair(q_ref[0, 0, hp], lane))

    def scores(jb, h):
        r0 = _block_start(qi - (band_blocks - 1) + jb, tk)
        b0 = _block_start(jb, tk)
        return _dot(k_ref[0, 0, h // 2, pl.ds(r0, tk), :], qm[h]) + bt_ref[h, pl.ds(b0, tk), :]

    def values(jb, h):
        return vt_scr[h, qi - (band_blocks - 1) + jb]

    first = jnp.maximum(band_blocks - 1 - qi, 0)
    _flash_attend(2 * n_pairs, first, band_blocks, scores, values, flash_scr)
    for hp in range(n_pairs):
        o_ref[0, hp] = _finish_pair(hp, g_ref[0, 0, hp], flash_scr[2])


def _chunk_attn(p1, bias_t, *, tq, pairs_per_step):
    _, B, HP, S, _ = p1.shape
    band = bias_t.shape[1]
    pp = pairs_per_step
    spec_q = lambda t: pl.BlockSpec((1, 1, pp, tq, LANES), lambda b, h, q, t=t: (t, b, h, q, 0))
    spec_s = lambda t: pl.BlockSpec((1, 1, pp, S, LANES), lambda b, h, q, t=t: (t, b, h, 0, 0),
                                    pipeline_mode=pl.Buffered(1))
    return pl.pallas_call(
        functools.partial(_chunk_kernel, tq=tq),
        grid=(B, HP // pp, S // tq),
        in_specs=[spec_q(0), spec_s(1), spec_s(2), spec_q(3),
                  pl.BlockSpec((2 * pp, band, tq), lambda b, h, q: (h, 0, 0), pipeline_mode=pl.Buffered(1))],
        out_specs=pl.BlockSpec((1, pp, tq, LANES), lambda b, h, q: (b, h, q, 0)),
        out_shape=jax.ShapeDtypeStruct((B, HP, S, LANES), BF16),
        scratch_shapes=[pltpu.VMEM((2 * pp, S // tq, VT_ROWS, tq), BF16)] + _flash_scratch(2 * pp, tq),
        compiler_params=_cparams(3),
        name="chunk_attn",
    )(p1, p1, p1, p1, bias_t)


def _band_bias_t(rel_table, *, tq):
    left = CHK_LEFT_CHUNKS * CHUNK
    pad = -(-left // tq) * tq
    rows = pad + tq
    r = jnp.arange(rows)[:, None]
    c = jnp.arange(tq)[None, :]
    back = (c + pad) // CHUNK - r // CHUNK
    in_band = (back >= 0) & (back <= CHK_LEFT_CHUNKS)
    period = rows + tq
    j = jnp.arange(period)
    c_minus_r = jnp.where(j < tq, j, j - period)
    bucket = jnp.clip(c_minus_r + pad, -MAX_REL_DIST, MAX_REL_DIST) + MAX_REL_DIST
    per_offset = rel_table.astype(F32)[:, bucket]
    n_heads = rel_table.shape[0]
    bias = jnp.tile(per_offset, (1, rows))[:, :rows * (period - 1)].reshape(n_heads, rows, period - 1)[:, :, :tq]
    return jnp.where(in_band[None], bias * LOG2E, NEG_INF)


def _out_kernel(*refs, n_in):
    a_refs = refs[:n_in]
    w_ref, x_ref, g_ref, o_ref = refs[n_in:]
    parts = []
    for a in a_refs:
        parts.extend(a[0, hp] for hp in range(a.shape[1]))
    att = jnp.concatenate(parts, axis=-1)
    y = jnp.dot(att, w_ref[...], preferred_element_type=F32)
    ms = jnp.mean(y * y, axis=-1, keepdims=True)
    o_ref[0] = x_ref[0] + (y * lax.rsqrt(ms + RMS_EPS)) * g_ref[...]


def _out_proj(atts, w, x, g, *, tm):
    B, S, D = x.shape
    n_in = len(atts)
    a_specs = [pl.BlockSpec((1, a.shape[1], tm, LANES), lambda b, s: (b, 0, s, 0)) for a in atts]
    return pl.pallas_call(
        functools.partial(_out_kernel, n_in=n_in),
        grid=(B, S // tm),
        in_specs=a_specs + [
            pl.BlockSpec(w.shape, lambda b, s: (0, 0)),
            pl.BlockSpec((1, tm, D), lambda b, s: (b, s, 0)),
            pl.BlockSpec((1, D), lambda b, s: (0, 0)),
        ],
        out_specs=pl.BlockSpec((1, tm, D), lambda b, s: (b, s, 0)),
        out_shape=jax.ShapeDtypeStruct((B, S, D), F32),
        compiler_params=_cparams(2),
        name=f"out_proj{n_in}",
    )(*atts, w, x, g)


def _rope_tiles(positions):
    inv_freq = ROPE_THETA ** (-jnp.arange(0, ROT_DIM, 2, dtype=F32) / ROT_DIM)
    ang = positions.astype(F32)[..., None] * inv_freq
    cos, sin = jnp.cos(ang), jnp.sin(ang)
    half = ROT_DIM // 2
    one = jnp.ones(ang.shape[:-1] + (HEAD_DIM - ROT_DIM,), F32)
    cos_head = jnp.concatenate([cos, cos, one], axis=-1)
    sin_head = jnp.concatenate([-sin, sin, 0.0 * one], axis=-1)
    assert cos_head.shape[-1] == HEAD_DIM and half * 2 == ROT_DIM
    return jnp.tile(cos_head, (1, 1, 2)), jnp.tile(sin_head, (1, 1, 2))


def _layer0_weights(w_in, b_forget, idx_k_g, idx_k_b):
    scale = HEAD_DIM ** -0.5
    off = 0
    cols = {}
    for name, n in (("fq", WIDTH), ("fk", WIDTH), ("fv", WIDTH), ("fl", FOX_HEADS), ("fg", WIDTH),
                    ("dq", WIDTH), ("dk", WIDTH), ("dv", WIDTH), ("iq", WIDTH), ("ik", IDX_DIM),
                    ("iw", IDX_HEADS), ("dg", WIDTH)):
        cols[name] = w_in[:, off:off + n]
        off += n
    assert off == w_in.shape[1]
    order = [None] * 9
    order[T_DQ], order[T_DK], order[T_IQ] = cols["dq"] * scale, cols["dk"], cols["iq"]
    order[T_FQ], order[T_FK], order[T_FV], order[T_FG] = cols["fq"] * scale, cols["fk"], cols["fv"], cols["fg"]
    order[T_DV], order[T_DG] = cols["dv"], cols["dg"]
    w_main = jnp.concatenate(order, axis=1).astype(BF16)
    pad = jnp.zeros((w_in.shape[0], LANES - L_IW - IDX_HEADS), w_in.dtype)
    w_small = jnp.concatenate([cols["ik"], cols["fl"], cols["iw"], pad], axis=1).astype(BF16)
    par = jnp.zeros((8, LANES), F32)
    par = par.at[0, L_FL:L_FL + FOX_HEADS].set(b_forget.astype(F32))
    par = par.at[1, :IDX_DIM].set(idx_k_g.astype(F32))
    par = par.at[2, :IDX_DIM].set(idx_k_b.astype(F32))
    return w_main, w_small, par


def _even_layer(x, cos_t, sin_t, pre_g, post_g, w_in, b_forget, idx_k_g, idx_k_b, w_out, top_k, cfg):
    w_main, w_small, par = _layer0_weights(w_in, b_forget, idx_k_g, idx_k_b)
    p0, ik2, small = _proj0(x, pre_g[None].astype(F32), w_main, w_small, cos_t, sin_t, par, tm=cfg["tm_proj"])
    fox = _fox(p0, small, tq=cfg["tq"])
    dsa = _dsa(p0, ik2, small, tq=cfg["tq"], top_k=top_k)
    return _out_proj([fox, dsa], w_out.astype(BF16), x, post_g[None].astype(F32), tm=cfg["tm_out"])


def _odd_layer(x, pre_g, post_g, w_in, rel_table, w_out, cfg):
    scale = HEAD_DIM ** -0.5
    width = CHK_HEADS * HEAD_DIM
    w = jnp.concatenate([w_in[:, :width] * scale, w_in[:, width:]], axis=1).astype(BF16)
    p1 = _proj1(x, pre_g[None].astype(F32), w, tm=cfg["tm_proj"], n_tensors=4)
    att = _chunk_attn(p1, _band_bias_t(rel_table, tq=cfg["tq"]), tq=cfg["tq"], pairs_per_step=4)
    return _out_proj([att], w_out.astype(BF16), x, post_g[None].astype(F32), tm=cfg["tm_out"])


def _config(seq):
    tq = 256
    assert seq % tq == 0 and tq % CHUNK == 0
    tm_proj = min(1024, seq)
    tm_out = min(512, seq)
    assert seq % tm_proj == 0 and seq % tm_out == 0
    return {"tq": tq, "tm_proj": tm_proj, "tm_out": tm_out}


def kernel(x, positions, pre_norm_g, post_norm_g, w_in_even, b_forget, idx_k_g, idx_k_b,
           w_out_even, w_in_odd, rel_bias, w_out_odd):
    seq = x.shape[1]
    depth = pre_norm_g.shape[0]
    cfg = _config(seq)
    top_k = min(DSA_TOPK, seq // 4)
    cos_t, sin_t = _rope_tiles(positions)
    for layer in range(depth):
        i = layer // 2
        if layer % 2 == 0:
            x = _even_layer(x, cos_t, sin_t, pre_norm_g[layer], post_norm_g[layer], w_in_even[i],
                            b_forget[i], idx_k_g[i], idx_k_b[i], w_out_even[i], top_k, cfg)
        else:
            x = _odd_layer(x, pre_norm_g[layer], post_norm_g[layer], w_in_odd[i], rel_bias[i], w_out_odd[i], cfg)
    return x
```

```python
import functools

import numpy as np
import jax
import jax.numpy as jnp
from jax import lax
from jax.experimental import pallas as pl
from jax.experimental.pallas import tpu as pltpu

F32 = jnp.float32
BF16 = jnp.bfloat16
I32 = jnp.int32

LANES = 128
MXU_COLS = 256
HEAD_DIM = 64
HALF = HEAD_DIM
ROT_DIM = HEAD_DIM // 4
ROPE_THETA = 500000.0
RMS_EPS = 1e-6
LN_EPS = 1e-6
NEG_INF = -1e30
LOG2E = 1.4426950408889634
CHUNK = 64
CHUNK_SHIFT = 6
FOX_HEADS = 8
DSA_HEADS = 8
IDX_HEADS = 8
IDX_DIM = 64
DSA_TOPK = 256
CHK_HEADS = 16
CHK_LEFT_CHUNKS = 8
MAX_REL_DIST = 128

WIDTH = 512
T_DQ, T_DK, T_IQ, T_FQ, T_FK, T_FV, T_FG, T_DV, T_DG = range(9)
TENSORS_PER_STEP = 3
L_IK = 0
L_FL = 64
L_IW = 72
L_AUG = 64

VMEM_LIMIT = 52 * 1024 * 1024

INT_MIN = -2147483648
I16 = jnp.int16
I16_MIN = -32768
HALF_BITS = 16
HALF_MASK = 0xFFFF
PACKED_ROWS = 16
COUNT_CHAINS = 4


def _sortable_key_of(value):
    b = int(np.array(value, np.float32).view(np.int32))
    return b ^ ((b >> 31) & 0x7FFFFFFF)


KEY_OF_NEG_INF = _sortable_key_of(NEG_INF)


def _cparams(n_axes):
    return pltpu.CompilerParams(dimension_semantics=("arbitrary",) * n_axes,
                                vmem_limit_bytes=VMEM_LIMIT)


def _dot(a, b):
    return jnp.dot(a, b, preferred_element_type=F32)


def _block_start(index, size):
    return index * size if isinstance(index, int) else pl.multiple_of(index * size, size)


def _rope(t, cos_t, sin_t, lane):
    first = (lane & (HALF - 1)) < (ROT_DIM // 2)
    partner = jnp.where(first, pltpu.roll(t, LANES - ROT_DIM // 2, 1), pltpu.roll(t, ROT_DIM // 2, 1))
    return t * cos_t + partner * sin_t


def _split3(x):
    hi = x.astype(BF16)
    r1 = x - hi.astype(F32)
    mid = r1.astype(BF16)
    lo = (r1 - mid.astype(F32)).astype(BF16)
    return hi, mid, lo


ROW_SPLIT = 2


def _project_step(h_scr, w_ref, p_ref, post, normalize_rows=None, after_normalize=None):
    tm = h_scr.shape[0]
    rows = tm // ROW_SPLIT
    n_chunks = w_ref.shape[1] // MXU_COLS
    per_chunk = MXU_COLS // LANES
    per_tensor = WIDTH // LANES

    def emit(g, c, r):
        for k in range(per_chunk):
            pair = c * per_chunk + k
            t, hp = pair // per_tensor, pair % per_tensor
            tile = post(g, t, r[:, k * LANES:(k + 1) * LANES])
            p_ref[t, 0, hp, g * rows:(g + 1) * rows, :] = tile.astype(BF16)

    if normalize_rows is not None:
        normalize_rows(0)
    pending = None
    for i, (g, c) in enumerate((g, c) for g in range(ROW_SPLIT) for c in range(n_chunks)):
        r = _dot(h_scr[g * rows:(g + 1) * rows, :], w_ref[:, c * MXU_COLS:(c + 1) * MXU_COLS])
        if i == 0 and normalize_rows is not None:
            for later in range(1, ROW_SPLIT):
                normalize_rows(later)
            if after_normalize is not None:
                after_normalize()
        if pending is not None:
            emit(*pending)
        pending = (g, c, r)
    emit(*pending)


def _rms_rows(x_ref, g_ref, h_scr, g):
    rows = h_scr.shape[0] // ROW_SPLIT
    xf = x_ref[0, g * rows:(g + 1) * rows, :]
    ms = jnp.mean(xf * xf, axis=-1, keepdims=True)
    h_scr[g * rows:(g + 1) * rows, :] = ((xf * lax.rsqrt(ms + RMS_EPS)) * g_ref[...]).astype(BF16)


def _proj0_kernel(x_ref, g_ref, w_ref, ws_ref, cos_ref, sin_ref, par_ref,
                  p_ref, ik_ref, sm_ref, h_scr, carry_scr, *, tm):
    s = pl.program_id(1)
    j = pl.program_id(2)
    lane = lax.broadcasted_iota(I32, (tm, LANES), 1)
    rows = tm // ROW_SPLIT
    lane_rows = lax.broadcasted_iota(I32, (rows, LANES), 1)

    def small_projection():
        small = _dot(h_scr[...], ws_ref[...])

        is_ik = lane < IDX_DIM
        mu = jnp.sum(jnp.where(is_ik, small, 0.0), axis=-1, keepdims=True) * (1.0 / IDX_DIM)
        xc = small - mu
        var = jnp.sum(jnp.where(is_ik, xc * xc, 0.0), axis=-1, keepdims=True) * (1.0 / IDX_DIM)
        y = xc * lax.rsqrt(var + LN_EPS) * par_ref[1:2, :] + par_ref[2:3, :]
        yr = _rope(y, cos_ref[0], sin_ref[0], lane)
        ik_ref[0] = jnp.where(is_ik, yr, pltpu.roll(yr, HALF, 1)).astype(BF16)

        z = small + par_ref[0:1, :]
        ls = jnp.minimum(z, 0.0) - jnp.log1p(jnp.exp(-jnp.abs(z)))
        is_f = (lane >= L_FL) & (lane < L_FL + FOX_HEADS)
        ls = jnp.where(is_f, ls, 0.0)
        is_iw = (lane >= L_IW) & (lane < L_IW + IDX_HEADS)
        other = small * jnp.where(is_iw, float((IDX_HEADS * IDX_DIM) ** -0.5), 1.0)

        @pl.when(s == 0)
        def _():
            carry_scr[...] = jnp.zeros_like(carry_scr)

        blk = LANES
        tri = (lax.broadcasted_iota(I32, (blk, blk), 0) >= lax.broadcasted_iota(I32, (blk, blk), 1))
        tri = jnp.where(tri, 1.0, 0.0).astype(BF16)
        lane_blk = lax.broadcasted_iota(I32, (blk, LANES), 1)
        is_f_blk = (lane_blk >= L_FL) & (lane_blk < L_FL + FOX_HEADS)
        cums = []
        for r in range(tm // blk):
            hi, mid, lo = _split3(ls[r * blk:(r + 1) * blk])
            cums.append(_dot(tri, hi) + _dot(tri, mid) + _dot(tri, lo))
        carry = carry_scr[0:1, :]
        for r in range(tm // blk):
            sm_ref[0, r * blk:(r + 1) * blk, :] = jnp.where(is_f_blk, cums[r] + carry, other[r * blk:(r + 1) * blk])
            carry = carry + cums[r][blk - 1:blk, :]
        carry_scr[0:1, :] = carry

    @pl.when(j == 0)
    def _first_step():
        def post(g, t, tile):
            sl = slice(g * rows, (g + 1) * rows)
            tile = tile * LOG2E if t == T_DQ else tile
            return _rope(tile, cos_ref[0, sl, :], sin_ref[0, sl, :], lane_rows)
        _project_step(h_scr, w_ref, p_ref, post,
                      normalize_rows=lambda g: _rms_rows(x_ref, g_ref, h_scr, g), after_normalize=small_projection)

    @pl.when(j > 0)
    def _later_steps():
        fq_scale = jnp.where(j == T_FQ // TENSORS_PER_STEP, LOG2E, 1.0)
        _project_step(h_scr, w_ref, p_ref, lambda g, t, tile: tile * fq_scale if t == T_FQ % TENSORS_PER_STEP else tile)


def _proj0(x, g, w_main, w_small, cos_t, sin_t, par, *, tm):
    B, S, D = x.shape
    n_tiles = w_main.shape[1] // WIDTH
    hp = WIDTH // LANES
    per = TENSORS_PER_STEP
    assert n_tiles % per == 0 and (T_DQ, T_DK, T_IQ) == (0, 1, 2) and T_FQ % per == 0
    return pl.pallas_call(
        functools.partial(_proj0_kernel, tm=tm),
        grid=(B, S // tm, n_tiles // per),
        in_specs=[
            pl.BlockSpec((1, tm, D), lambda b, s, j: (b, s, 0)),
            pl.BlockSpec((1, D), lambda b, s, j: (0, 0)),
            pl.BlockSpec((D, per * WIDTH), lambda b, s, j: (0, j)),
            pl.BlockSpec((D, LANES), lambda b, s, j: (0, 0)),
            pl.BlockSpec((1, tm, LANES), lambda b, s, j: (b, s, 0)),
            pl.BlockSpec((1, tm, LANES), lambda b, s, j: (b, s, 0)),
            pl.BlockSpec((8, LANES), lambda b, s, j: (0, 0)),
        ],
        out_specs=[
            pl.BlockSpec((per, 1, hp, tm, LANES), lambda b, s, j: (j, b, 0, s, 0)),
            pl.BlockSpec((1, tm, LANES), lambda b, s, j: (b, s, 0)),
            pl.BlockSpec((1, tm, LANES), lambda b, s, j: (b, s, 0)),
        ],
        out_shape=[
            jax.ShapeDtypeStruct((n_tiles, B, hp, S, LANES), BF16),
            jax.ShapeDtypeStruct((B, S, LANES), BF16),
            jax.ShapeDtypeStruct((B, S, LANES), F32),
        ],
        scratch_shapes=[pltpu.VMEM((tm, D), BF16), pltpu.VMEM((8, LANES), F32)],
        compiler_params=_cparams(3),
        name="proj0",
    )(x, g, w_main, w_small, cos_t, sin_t, par)


def _proj1_kernel(x_ref, g_ref, w_ref, p_ref, h_scr, *, q_tiles):
    j = pl.program_id(2)

    @pl.when(j == 0)
    def _first_step():
        _project_step(h_scr, w_ref, p_ref, lambda g, t, tile: tile * LOG2E if t < q_tiles else tile,
                      normalize_rows=lambda g: _rms_rows(x_ref, g_ref, h_scr, g))

    @pl.when(j > 0)
    def _later_steps():
        _project_step(h_scr, w_ref, p_ref, lambda g, t, tile: tile)


def _proj1(x, g, w, *, tm, n_tensors):
    B, S, D = x.shape
    n_tiles = w.shape[1] // WIDTH
    per = n_tiles // n_tensors
    hp = WIDTH // LANES
    step = 2 * per
    return pl.pallas_call(
        functools.partial(_proj1_kernel, q_tiles=per),
        grid=(B, S // tm, n_tiles // step),
        in_specs=[
            pl.BlockSpec((1, tm, D), lambda b, s, j: (b, s, 0)),
            pl.BlockSpec((1, D), lambda b, s, j: (0, 0)),
            pl.BlockSpec((D, step * WIDTH), lambda b, s, j: (0, j)),
        ],
        out_specs=pl.BlockSpec((step, 1, hp, tm, LANES), lambda b, s, j: (j, b, 0, s, 0)),
        out_shape=jax.ShapeDtypeStruct((n_tiles, B, hp, S, LANES), BF16),
        scratch_shapes=[pltpu.VMEM((tm, D), BF16)],
        compiler_params=_cparams(3),
        name="proj1",
    )(x, g, w)


VT_ROWS = HEAD_DIM + PACKED_ROWS


def _flash_scratch(n_heads, tq):
    return [pltpu.VMEM((n_heads, tq, tq), F32), pltpu.VMEM((n_heads, 1, tq), F32),
            pltpu.VMEM((n_heads, VT_ROWS, tq), F32)]


def _store_values_t(vt_scr, hp, c, v_blk):
    v_t = v_blk.astype(F32).T
    tk = v_t.shape[1]
    ones_row = jnp.where(lax.broadcasted_iota(I32, (PACKED_ROWS, tk), 0) == 0, 1.0, 0.0).astype(BF16)
    for hh in range(2):
        vt_scr[2 * hp + hh, c, 0:HEAD_DIM, :] = v_t[hh * HEAD_DIM:(hh + 1) * HEAD_DIM].astype(BF16)
        vt_scr[2 * hp + hh, c, HEAD_DIM:VT_ROWS, :] = ones_row


def _flash_update(h, s_t, vt, m_scr, acc_scr):
    m = m_scr[h]
    m_new = jnp.maximum(m, jnp.max(s_t, axis=0, keepdims=True))
    alpha = jnp.exp2(m - m_new)
    p = jnp.exp2((s_t - m_new).astype(BF16))
    m_scr[h] = m_new
    acc_scr[h] = alpha * acc_scr[h] + _dot(vt, p)


def _flash_attend(n_heads, lo, hi, scores_fn, values_fn, scratch, last_scores_fixup=None):
    s_scr, m_scr, acc_scr = scratch
    m_scr[...] = jnp.full(m_scr.shape, NEG_INF, F32)
    acc_scr[...] = jnp.zeros(acc_scr.shape, F32)
    for h in range(n_heads):
        s_scr[h] = scores_fn(lo, h)

    def body(kb, _):
        for h in range(n_heads):
            s_t = s_scr[h]
            s_scr[h] = scores_fn(kb + 1, h)
            _flash_update(h, s_t, values_fn(kb, h), m_scr, acc_scr)
        return 0
    lax.fori_loop(lo, hi - 1, body, 0)

    for h in range(n_heads):
        s_t = s_scr[h]
        if last_scores_fixup is not None:
            s_t = last_scores_fixup(s_t)
        _flash_update(h, s_t, values_fn(hi - 1, h), m_scr, acc_scr)


def _finish_pair(hp, gate, acc_scr):
    heads = [acc_scr[2 * hp + hh, 0:HEAD_DIM, :] / acc_scr[2 * hp + hh, HEAD_DIM:HEAD_DIM + 1, :] for hh in range(2)]
    g = gate.astype(F32)
    return (jnp.concatenate(heads, axis=0).T * (g * jax.nn.sigmoid(g))).astype(BF16)


def _split_pair(q, lane):
    qf = q.astype(F32)
    lo = lane < HALF
    return jnp.where(lo, qf, 0.0).T.astype(BF16), jnp.where(lo, 0.0, qf).T.astype(BF16)


def _fox_aug(base, f_col, lane, is_key):
    hi, mid, lo = (t.astype(F32) for t in _split3(f_col))
    if is_key:
        a, b, c, d = -hi, -mid, -lo, 1.0
        tail = jnp.where(lane < L_AUG + 6, d, 0.0)
        aug = jnp.where(lane == L_AUG, a, jnp.where(lane == L_AUG + 1, b, jnp.where(lane == L_AUG + 2, c, tail)))
    else:
        tail = jnp.where(lane == L_AUG + 3, hi, jnp.where(lane == L_AUG + 4, mid,
                                                         jnp.where(lane == L_AUG + 5, lo, 0.0)))
        aug = jnp.where(lane < L_AUG + 3, 1.0, tail)
    out = jnp.where(lane < L_AUG, base, aug)
    return out.astype(BF16) if is_key else out.T.astype(BF16)


def _fox_kernel(q_ref, k_ref, v_ref, g_ref, f_ref, o_ref, ka_scr, vt_scr, *flash_scr, tq):
    qi = pl.program_id(1)
    tk = tq
    n_pairs = k_ref.shape[2]
    n_blocks = k_ref.shape[3] // tk
    lane = lax.broadcasted_iota(I32, (tq, LANES), 1)

    def head_f(f_tile, h):
        return jnp.sum(jnp.where(lane == L_FL + h, f_tile, 0.0), axis=-1, keepdims=True) * LOG2E

    def head_base(t, hh):
        return t if hh == 0 else pltpu.roll(t, HALF, 1)

    @pl.when(qi == 0)
    def _prepare():
        def body(c, _):
            r0 = pl.multiple_of(c * tk, tk)
            ff = f_ref[0, pl.ds(r0, tk), :]
            for hp in range(n_pairs):
                kf = k_ref[0, 0, hp, pl.ds(r0, tk), :].astype(F32)
                for hh in range(2):
                    h = 2 * hp + hh
                    ka_scr[h, pl.ds(r0, tk), :] = _fox_aug(head_base(kf, hh), head_f(ff, h), lane, True)
                _store_values_t(vt_scr, hp, c, v_ref[0, 0, hp, pl.ds(r0, tk), :])
            return 0
        lax.fori_loop(0, n_blocks, body, 0)

    q0 = pl.multiple_of(qi * tq, tq)
    fq = f_ref[0, pl.ds(q0, tq), :]
    qa = []
    for hp in range(n_pairs):
        qf = q_ref[0, 0, hp].astype(F32)
        qa.extend(_fox_aug(head_base(qf, hh), head_f(fq, 2 * hp + hh), lane, False) for hh in range(2))

    def scores(kb, h):
        return _dot(ka_scr[h, pl.ds(_block_start(kb, tk), tk), :], qa[h])

    def values(kb, h):
        return vt_scr[h, kb]

    def causal(s_t):
        future = lax.broadcasted_iota(I32, (tk, tq), 0) > lax.broadcasted_iota(I32, (tk, tq), 1)
        return jnp.where(future, NEG_INF, s_t)

    _flash_attend(2 * n_pairs, 0, qi + 1, scores, values, flash_scr, last_scores_fixup=causal)
    for hp in range(n_pairs):
        o_ref[0, hp] = _finish_pair(hp, g_ref[0, 0, hp], flash_scr[2])


def _fox(p0, small, *, tq):
    _, B, HP, S, _ = p0.shape
    spec_q = lambda t: pl.BlockSpec((1, 1, HP, tq, LANES), lambda b, q, t=t: (t, b, 0, q, 0))
    spec_s = lambda t: pl.BlockSpec((1, 1, HP, S, LANES), lambda b, q, t=t: (t, b, 0, 0, 0),
                                    pipeline_mode=pl.Buffered(1))
    return pl.pallas_call(
        functools.partial(_fox_kernel, tq=tq),
        grid=(B, S // tq),
        in_specs=[spec_q(T_FQ), spec_s(T_FK), spec_s(T_FV), spec_q(T_FG),
                  pl.BlockSpec((1, S, LANES), lambda b, q: (b, 0, 0), pipeline_mode=pl.Buffered(1))],
        out_specs=pl.BlockSpec((1, HP, tq, LANES), lambda b, q: (b, 0, q, 0)),
        out_shape=jax.ShapeDtypeStruct((B, HP, S, LANES), BF16),
        scratch_shapes=[pltpu.VMEM((2 * HP, S, LANES), BF16), pltpu.VMEM((2 * HP, S // tq, VT_ROWS, tq), BF16)]
        + _flash_scratch(2 * HP, tq),
        compiler_params=_cparams(2),
        name="fox_attn",
    )(p0, p0, p0, p0, small)


def _dsa_kernel(q_ref, k_ref, v_ref, iq_ref, g_ref, ik_ref, sm_ref, o_ref, vt_scr, key_scr, hi_scr, lo_scr,
                *flash_scr, tq, top_k):
    qi = pl.program_id(1)
    tk = tq
    n_pairs = k_ref.shape[2]
    seq = k_ref.shape[3]
    n_blocks = seq // tk
    lane = lax.broadcasted_iota(I32, (tq, LANES), 1)

    @pl.when(qi == 0)
    def _prepare():
        def body(c, _):
            r0 = pl.multiple_of(c * tk, tk)
            for hp in range(n_pairs):
                _store_values_t(vt_scr, hp, c, v_ref[0, 0, hp, pl.ds(r0, tk), :])
            return 0
        lax.fori_loop(0, n_blocks, body, 0)

    iw_t = sm_ref[0].T[L_IW:L_IW + IDX_HEADS, :]
    iq_m = []
    for hp in range(IDX_HEADS // 2):
        iq_m.extend(_split_pair(iq_ref[0, 0, hp], lane))

    def index_keys(r0):
        ikb = ik_ref[0, pl.ds(r0, tk), :]
        score = jnp.zeros((tk, tq), F32)
        for h in range(IDX_HEADS):
            score = score + jnp.maximum(_dot(ikb, iq_m[h]), 0.0) * iw_t[h:h + 1, :]
        bits = pltpu.bitcast(score, I32)
        return bits ^ ((bits >> 31) & 0x7FFFFFFF)

    def store_keys(r0, key):
        key_scr[pl.ds(r0, tk), :] = key
        hi_scr[pl.ds(r0, tk), :] = (key >> HALF_BITS).astype(I16)
        lo_scr[pl.ds(r0, tk), :] = ((key & HALF_MASK) + I16_MIN).astype(I16)

    def fill(kb, _):
        r0 = pl.multiple_of(kb * tk, tk)
        store_keys(r0, index_keys(r0))
        return 0
    lax.fori_loop(0, qi, fill, 0)

    d0 = pl.multiple_of(qi * tk, tk)
    row = lax.broadcasted_iota(I32, (tk, tq), 0)
    col = lax.broadcasted_iota(I32, (tk, tq), 1)
    beyond_chunk = (row >> CHUNK_SHIFT) > (col >> CHUNK_SHIFT)
    store_keys(d0, jnp.where(beyond_chunk, INT_MIN, index_keys(d0)))
    pad0 = pl.multiple_of((qi + 1) * tk, tk)
    hi_scr[pl.ds(pad0, tk), :] = jnp.full((tk, tq), I16_MIN, I16)
    lo_scr[pl.ds(pad0, tk), :] = jnp.full((tk, tq), I16_MIN, I16)

    q_pos = qi * tq + lax.broadcasted_iota(I32, (1, tq), 1)
    n_beyond = (seq - ((q_pos >> CHUNK_SHIFT) + 1) * CHUNK).astype(F32)
    neg_hi = KEY_OF_NEG_INF >> HALF_BITS
    neg_lo = (KEY_OF_NEG_INF & HALF_MASK) + I16_MIN

    rows16 = 2 * tk // PACKED_ROWS

    def count16(ref, pred, t_s):
        t16 = t_s.astype(I16)

        def body(i, accs):
            r0 = pl.multiple_of(i * 2 * tk, 2 * tk)
            one = jnp.where(pred(ref[pl.ds(r0, 2 * tk), :], t16), jnp.int16(1), jnp.int16(0))
            accs = list(accs)
            for g in range(rows16):
                accs[g % len(accs)] = accs[g % len(accs)] + one[g * PACKED_ROWS:(g + 1) * PACKED_ROWS]
            return tuple(accs)
        zero = jnp.zeros((PACKED_ROWS, tq), I16)
        accs = lax.fori_loop(0, (qi + 2) // 2, body, (zero,) * COUNT_CHAINS)
        return jnp.sum(sum(accs[1:], accs[0]).astype(F32), axis=0, keepdims=True)

    def search16(ref, need, beyond, beyond_key):
        def bit_step(i, t_u):
            cand_u = t_u | lax.shift_left(jnp.int32(1), HALF_BITS - 1 - i)
            cand_s = cand_u + I16_MIN
            c = count16(ref, lambda k, t: k >= t, cand_s) + jnp.where(cand_s <= beyond_key, beyond, 0.0)
            return jnp.where(c >= need, cand_u, t_u)
        return lax.fori_loop(0, HALF_BITS, bit_step, jnp.zeros((1, tq), I32)) + I16_MIN

    thr_hi = search16(hi_scr, float(top_k), n_beyond, neg_hi)
    n_above_hi = count16(hi_scr, lambda k, t: k > t, thr_hi) + jnp.where(thr_hi < neg_hi, n_beyond, 0.0)

    thr_hi16 = thr_hi.astype(I16)

    def keep_low(i, _):
        r0 = pl.multiple_of(i * 2 * tk, 2 * tk)
        sl = pl.ds(r0, 2 * tk)
        lo_scr[sl, :] = jnp.where(hi_scr[sl, :] == thr_hi16, lo_scr[sl, :], jnp.int16(I16_MIN))
        return 0
    lax.fori_loop(0, (qi + 2) // 2, keep_low, 0)

    beyond_lo = jnp.where(thr_hi == neg_hi, n_beyond, 0.0)
    thr_lo = search16(lo_scr, top_k - n_above_hi, beyond_lo, neg_lo)
    n_above_lo = count16(lo_scr, lambda k, t: k > t, thr_lo)
    n_ties = count16(lo_scr, lambda k, t: k >= t, thr_lo) - n_above_lo
    n_above = n_above_hi + n_above_lo + jnp.where(thr_lo < neg_lo, beyond_lo, 0.0)
    thr = lax.shift_left(thr_hi, HALF_BITS) | (thr_lo - I16_MIN)
    n_ties_kept = top_k - n_above
    must_rank_ties = jnp.max(jnp.where(n_ties > n_ties_kept, 1.0, 0.0)) > 0.5

    @pl.when(must_rank_ties)
    def _mask_with_tie_ranks():
        strict_lower = jnp.where(row > col, 1.0, 0.0).astype(BF16)

        def make_bias(kb, ties_before):
            r0 = pl.multiple_of(kb * tk, tk)
            kblk = key_scr[pl.ds(r0, tk), :]
            eq = kblk == thr
            eq_f = jnp.where(eq, 1.0, 0.0)
            rank = jnp.dot(strict_lower, eq_f.astype(BF16), preferred_element_type=F32) + ties_before
            bias = jnp.where(kblk > thr, 0.0, jnp.where(eq, jnp.where(rank < n_ties_kept, 0.0, NEG_INF), NEG_INF))
            key_scr[pl.ds(r0, tk), :] = pltpu.bitcast(bias, I32)
            return ties_before + jnp.sum(eq_f, axis=0, keepdims=True)
        lax.fori_loop(0, qi + 1, make_bias, jnp.zeros((1, tq), F32))

    @pl.when(jnp.logical_not(must_rank_ties))
    def _mask_all_ties_kept():
        def make_bias(kb, _):
            r0 = pl.multiple_of(kb * tk, tk)
            bias = jnp.where(key_scr[pl.ds(r0, tk), :] >= thr, 0.0, NEG_INF)
            key_scr[pl.ds(r0, tk), :] = pltpu.bitcast(bias, I32)
            return 0
        lax.fori_loop(0, qi + 1, make_bias, 0)

    qm = []
    for hp in range(n_pairs):
        qm.extend(_split_pair(q_ref[0, 0, hp], lane))

    def scores(kb, h):
        r0 = _block_start(kb, tk)
        bias = pltpu.bitcast(key_scr[pl.ds(r0, tk), :], F32)
        return _dot(k_ref[0, 0, h // 2, pl.ds(r0, tk), :], qm[h]) + bias

    def values(kb, h):
        return vt_scr[h, kb]

    _flash_attend(2 * n_pairs, 0, qi + 1, scores, values, flash_scr)
    for hp in range(n_pairs):
        o_ref[0, hp] = _finish_pair(hp, g_ref[0, 0, hp], flash_scr[2])


def _dsa(p0, ik2, small, *, tq, top_k):
    _, B, HP, S, _ = p0.shape
    spec_q = lambda t: pl.BlockSpec((1, 1, HP, tq, LANES), lambda b, q, t=t: (t, b, 0, q, 0))
    spec_s = lambda t: pl.BlockSpec((1, 1, HP, S, LANES), lambda b, q, t=t: (t, b, 0, 0, 0),
                                    pipeline_mode=pl.Buffered(1))
    return pl.pallas_call(
        functools.partial(_dsa_kernel, tq=tq, top_k=top_k),
        grid=(B, S // tq),
        in_specs=[spec_q(T_DQ), spec_s(T_DK), spec_s(T_DV), spec_q(T_IQ), spec_q(T_DG),
                  pl.BlockSpec((1, S, LANES), lambda b, q: (b, 0, 0), pipeline_mode=pl.Buffered(1)),
                  pl.BlockSpec((1, tq, LANES), lambda b, q: (b, q, 0))],
        out_specs=pl.BlockSpec((1, HP, tq, LANES), lambda b, q: (b, 0, q, 0)),
        out_shape=jax.ShapeDtypeStruct((B, HP, S, LANES), BF16),
        scratch_shapes=[pltpu.VMEM((2 * HP, S // tq, VT_ROWS, tq), BF16), pltpu.VMEM((S, tq), I32),
                        pltpu.VMEM((S + tq, tq), I16), pltpu.VMEM((S + tq, tq), I16)]
        + _flash_scratch(2 * HP, tq),
        compiler_params=_cparams(2),
        name="dsa_attn",
    )(p0, p0, p0, p0, p0, ik2, small)


def _chunk_kernel(q_ref, k_ref, v_ref, g_ref, bt_ref, o_ref, vt_scr, *flash_scr, tq):
    qi = pl.program_id(2)
    tk = tq
    n_pairs = k_ref.shape[2]
    n_blocks = k_ref.shape[3] // tk
    band_blocks = bt_ref.shape[1] // tk
    lane = lax.broadcasted_iota(I32, (tq, LANES), 1)

    @pl.when(qi == 0)
    def _prepare():
        def body(c, _):
            r0 = pl.multiple_of(c * tk, tk)
            for hp in range(n_pairs):
                _store_values_t(vt_scr, hp, c, v_ref[0, 0, hp, pl.ds(r0, tk), :])
            return 0
        lax.fori_loop(0, n_blocks, body, 0)

    qm = []
    for hp in range(n_pairs):
        qm.extend(_split_pair(q_ref[0, 0, hp], lane))

    def scores(jb, h):
        r0 = _block_start(qi - (band_blocks - 1) + jb, tk)
        b0 = _block_start(jb, tk)
        return _dot(k_ref[0, 0, h // 2, pl.ds(r0, tk), :], qm[h]) + bt_ref[h, pl.ds(b0, tk), :]

    def values(jb, h):
        return vt_scr[h, qi - (band_blocks - 1) + jb]

    first = jnp.maximum(band_blocks - 1 - qi, 0)
    _flash_attend(2 * n_pairs, first, band_blocks, scores, values, flash_scr)
    for hp in range(n_pairs):
        o_ref[0, hp] = _finish_pair(hp, g_ref[0, 0, hp], flash_scr[2])


def _chunk_attn(p1, bias_t, *, tq, n_tensors):
    n_tiles, B, pp, S, _ = p1.shape
    groups = n_tiles // n_tensors
    HP = groups * pp
    band = bias_t.shape[1]
    spec_q = lambda t: pl.BlockSpec((1, 1, pp, tq, LANES), lambda b, h, q, t=t: (t * groups + h, b, 0, q, 0))
    spec_s = lambda t: pl.BlockSpec((1, 1, pp, S, LANES), lambda b, h, q, t=t: (t * groups + h, b, 0, 0, 0),
                                    pipeline_mode=pl.Buffered(1))
    return pl.pallas_call(
        functools.partial(_chunk_kernel, tq=tq),
        grid=(B, groups, S // tq),
        in_specs=[spec_q(0), spec_s(1), spec_s(2), spec_q(3),
                  pl.BlockSpec((2 * pp, band, tq), lambda b, h, q: (h, 0, 0), pipeline_mode=pl.Buffered(1))],
        out_specs=pl.BlockSpec((1, pp, tq, LANES), lambda b, h, q: (b, h, q, 0)),
        out_shape=jax.ShapeDtypeStruct((B, HP, S, LANES), BF16),
        scratch_shapes=[pltpu.VMEM((2 * pp, S // tq, VT_ROWS, tq), BF16)] + _flash_scratch(2 * pp, tq),
        compiler_params=_cparams(3),
        name="chunk_attn",
    )(p1, p1, p1, p1, bias_t)


def _band_bias_t(rel_table, *, tq):
    left = CHK_LEFT_CHUNKS * CHUNK
    pad = -(-left // tq) * tq
    rows = pad + tq
    r = jnp.arange(rows)[:, None]
    c = jnp.arange(tq)[None, :]
    back = (c + pad) // CHUNK - r // CHUNK
    in_band = (back >= 0) & (back <= CHK_LEFT_CHUNKS)
    period = rows + tq
    j = jnp.arange(period)
    c_minus_r = jnp.where(j < tq, j, j - period)
    bucket = jnp.clip(c_minus_r + pad, -MAX_REL_DIST, MAX_REL_DIST) + MAX_REL_DIST
    per_offset = rel_table.astype(F32)[:, bucket]
    n_heads = rel_table.shape[0]
    bias = jnp.tile(per_offset, (1, rows))[:, :rows * (period - 1)].reshape(n_heads, rows, period - 1)[:, :, :tq]
    return jnp.where(in_band[None], bias * LOG2E, NEG_INF)


def _out_kernel(*refs, n_in):
    a_refs = refs[:n_in]
    w_ref, x_ref, g_ref, o_ref = refs[n_in:]
    parts = []
    for a in a_refs:
        parts.extend(a[0, hp] for hp in range(a.shape[1]))
    att = jnp.concatenate(parts, axis=-1)
    y = jnp.dot(att, w_ref[...], preferred_element_type=F32)
    ms = jnp.mean(y * y, axis=-1, keepdims=True)
    o_ref[0] = x_ref[0] + (y * lax.rsqrt(ms + RMS_EPS)) * g_ref[...]


def _out_proj(atts, w, x, g, *, tm):
    B, S, D = x.shape
    n_in = len(atts)
    a_specs = [pl.BlockSpec((1, a.shape[1], tm, LANES), lambda b, s: (b, 0, s, 0)) for a in atts]
    return pl.pallas_call(
        functools.partial(_out_kernel, n_in=n_in),
        grid=(B, S // tm),
        in_specs=a_specs + [
            pl.BlockSpec(w.shape, lambda b, s: (0, 0)),
            pl.BlockSpec((1, tm, D), lambda b, s: (b, s, 0)),
            pl.BlockSpec((1, D), lambda b, s: (0, 0)),
        ],
        out_specs=pl.BlockSpec((1, tm, D), lambda b, s: (b, s, 0)),
        out_shape=jax.ShapeDtypeStruct((B, S, D), F32),
        compiler_params=_cparams(2),
        name=f"out_proj{n_in}",
    )(*atts, w, x, g)


def _rope_tiles(positions):
    inv_freq = ROPE_THETA ** (-jnp.arange(0, ROT_DIM, 2, dtype=F32) / ROT_DIM)
    ang = positions.astype(F32)[..., None] * inv_freq
    cos, sin = jnp.cos(ang), jnp.sin(ang)
    half = ROT_DIM // 2
    one = jnp.ones(ang.shape[:-1] + (HEAD_DIM - ROT_DIM,), F32)
    cos_head = jnp.concatenate([cos, cos, one], axis=-1)
    sin_head = jnp.concatenate([-sin, sin, 0.0 * one], axis=-1)
    assert cos_head.shape[-1] == HEAD_DIM and half * 2 == ROT_DIM
    return jnp.tile(cos_head, (1, 1, 2)), jnp.tile(sin_head, (1, 1, 2))


def _layer0_weights(w_in, b_forget, idx_k_g, idx_k_b):
    scale = HEAD_DIM ** -0.5
    off = 0
    cols = {}
    for name, n in (("fq", WIDTH), ("fk", WIDTH), ("fv", WIDTH), ("fl", FOX_HEADS), ("fg", WIDTH),
                    ("dq", WIDTH), ("dk", WIDTH), ("dv", WIDTH), ("iq", WIDTH), ("ik", IDX_DIM),
                    ("iw", IDX_HEADS), ("dg", WIDTH)):
        cols[name] = w_in[:, off:off + n]
        off += n
    assert off == w_in.shape[1]
    order = [None] * 9
    order[T_DQ], order[T_DK], order[T_IQ] = cols["dq"] * scale, cols["dk"], cols["iq"]
    order[T_FQ], order[T_FK], order[T_FV], order[T_FG] = cols["fq"] * scale, cols["fk"], cols["fv"], cols["fg"]
    order[T_DV], order[T_DG] = cols["dv"], cols["dg"]
    w_main = jnp.concatenate(order, axis=1).astype(BF16)
    pad = jnp.zeros((w_in.shape[0], LANES - L_IW - IDX_HEADS), w_in.dtype)
    w_small = jnp.concatenate([cols["ik"], cols["fl"], cols["iw"], pad], axis=1).astype(BF16)
    par = jnp.zeros((8, LANES), F32)
    par = par.at[0, L_FL:L_FL + FOX_HEADS].set(b_forget.astype(F32))
    par = par.at[1, :IDX_DIM].set(idx_k_g.astype(F32))
    par = par.at[2, :IDX_DIM].set(idx_k_b.astype(F32))
    return w_main, w_small, par


def _even_layer(x, cos_t, sin_t, pre_g, post_g, w_in, b_forget, idx_k_g, idx_k_b, w_out, top_k, cfg):
    w_main, w_small, par = _layer0_weights(w_in, b_forget, idx_k_g, idx_k_b)
    p0, ik2, small = _proj0(x, pre_g[None].astype(F32), w_main, w_small, cos_t, sin_t, par, tm=cfg["tm_proj"])
    fox = _fox(p0, small, tq=cfg["tq"])
    dsa = _dsa(p0, ik2, small, tq=cfg["tq"], top_k=top_k)
    return _out_proj([fox, dsa], w_out.astype(BF16), x, post_g[None].astype(F32), tm=cfg["tm_out"])


def _odd_layer(x, pre_g, post_g, w_in, rel_table, w_out, cfg):
    scale = HEAD_DIM ** -0.5
    width = CHK_HEADS * HEAD_DIM
    w = jnp.concatenate([w_in[:, :width] * scale, w_in[:, width:]], axis=1).astype(BF16)
    p1 = _proj1(x, pre_g[None].astype(F32), w, tm=cfg["tm_proj"], n_tensors=4)
    att = _chunk_attn(p1, _band_bias_t(rel_table, tq=cfg["tq"]), tq=cfg["tq"], n_tensors=4)
    return _out_proj([att], w_out.astype(BF16), x, post_g[None].astype(F32), tm=cfg["tm_out"])


def _config(seq):
    tq = 256
    assert seq % tq == 0 and tq % CHUNK == 0
    tm_proj = min(1024, seq)
    tm_out = min(512, seq)
    assert seq % tm_proj == 0 and seq % tm_out == 0
    return {"tq": tq, "tm_proj": tm_proj, "tm_out": tm_out}


def kernel(x, positions, pre_norm_g, post_norm_g, w_in_even, b_forget, idx_k_g, idx_k_b,
           w_out_even, w_in_odd, rel_bias, w_out_odd):
    seq = x.shape[1]
    depth = pre_norm_g.shape[0]
    cfg = _config(seq)
    top_k = min(DSA_TOPK, seq // 4)
    cos_t, sin_t = _rope_tiles(positions)
    for layer in range(depth):
        i = layer // 2
        if layer % 2 == 0:
            x = _even_layer(x, cos_t, sin_t, pre_norm_g[layer], post_norm_g[layer], w_in_even[i],
                            b_forget[i], idx_k_g[i], idx_k_b[i], w_out_even[i], top_k, cfg)
        else:
            x = _odd_layer(x, pre_norm_g[layer], post_norm_g[layer], w_in_odd[i], rel_bias[i], w_out_odd[i], cfg)
    return x
```

```python
import functools

import numpy as np
import jax
import jax.numpy as jnp
from jax import lax
from jax.experimental import pallas as pl
from jax.experimental.pallas import tpu as pltpu

F32 = jnp.float32
BF16 = jnp.bfloat16
I32 = jnp.int32

LANES = 128
LANE_SHIFT = 7
MXU_COLS = 256
HEAD_DIM = 64
HALF = HEAD_DIM
ROT_DIM = HEAD_DIM // 4
ROPE_THETA = 500000.0
RMS_EPS = 1e-6
LN_EPS = 1e-6
NEG_INF = -1e30
LOG2E = 1.4426950408889634
CHUNK = 64
CHUNK_SHIFT = 6
FOX_HEADS = 8
DSA_HEADS = 8
IDX_HEADS = 8
IDX_DIM = 64
DSA_TOPK = 256
CHK_HEADS = 16
CHK_LEFT_CHUNKS = 8
MAX_REL_DIST = 128

WIDTH = 512
T_DQ, T_DK, T_IQ, T_FQ, T_FK, T_FV, T_FG, T_DV, T_DG = range(9)
TENSORS_PER_STEP = 3
L_IK = 0
L_FL = 64
L_IW = 72
L_AUG = 64

VMEM_LIMIT = 52 * 1024 * 1024

INT_MIN = -2147483648
I16 = jnp.int16
I16_MIN = -32768
HALF_BITS = 16
HALF_MASK = 0xFFFF
PACKED_ROWS = 16
COUNT_CHAINS = 4


def _sortable_key_of(value):
    b = int(np.array(value, np.float32).view(np.int32))
    return b ^ ((b >> 31) & 0x7FFFFFFF)


KEY_OF_NEG_INF = _sortable_key_of(NEG_INF)


def _cparams(n_axes):
    return pltpu.CompilerParams(dimension_semantics=("arbitrary",) * n_axes,
                                vmem_limit_bytes=VMEM_LIMIT)


def _dot(a, b):
    return jnp.dot(a, b, preferred_element_type=F32)


def _block_start(index, size):
    return index * size if isinstance(index, int) else pl.multiple_of(index * size, size)


def _rope(t, cos_t, sin_t, lane):
    first = (lane & (HALF - 1)) < (ROT_DIM // 2)
    partner = jnp.where(first, pltpu.roll(t, LANES - ROT_DIM // 2, 1), pltpu.roll(t, ROT_DIM // 2, 1))
    return t * cos_t + partner * sin_t


def _split3(x):
    hi = x.astype(BF16)
    r1 = x - hi.astype(F32)
    mid = r1.astype(BF16)
    lo = (r1 - mid.astype(F32)).astype(BF16)
    return hi, mid, lo


ROW_SPLIT = 2


def _project_step(h_scr, w_ref, p_ref, post, normalize_rows=None, after_normalize=None):
    tm = h_scr.shape[0]
    rows = tm // ROW_SPLIT
    n_chunks = w_ref.shape[1] // MXU_COLS
    per_chunk = MXU_COLS // LANES
    per_tensor = WIDTH // LANES

    def emit(g, c, r):
        for k in range(per_chunk):
            pair = c * per_chunk + k
            t, hp = pair // per_tensor, pair % per_tensor
            tile = post(g, t, r[:, k * LANES:(k + 1) * LANES])
            p_ref[t, 0, hp, g * rows:(g + 1) * rows, :] = tile.astype(BF16)

    if normalize_rows is not None:
        normalize_rows(0)
    pending = None
    for i, (g, c) in enumerate((g, c) for g in range(ROW_SPLIT) for c in range(n_chunks)):
        r = _dot(h_scr[g * rows:(g + 1) * rows, :], w_ref[:, c * MXU_COLS:(c + 1) * MXU_COLS])
        if i == 0 and normalize_rows is not None:
            for later in range(1, ROW_SPLIT):
                normalize_rows(later)
            if after_normalize is not None:
                after_normalize()
        if pending is not None:
            emit(*pending)
        pending = (g, c, r)
    emit(*pending)


def _rms_rows(x_ref, g_ref, h_scr, g):
    rows = h_scr.shape[0] // ROW_SPLIT
    xf = x_ref[0, g * rows:(g + 1) * rows, :]
    ms = jnp.mean(xf * xf, axis=-1, keepdims=True)
    h_scr[g * rows:(g + 1) * rows, :] = ((xf * lax.rsqrt(ms + RMS_EPS)) * g_ref[...]).astype(BF16)


def _proj0_kernel(x_ref, g_ref, w_ref, ws_ref, cos_ref, sin_ref, par_ref,
                  p_ref, ik_ref, sm_ref, h_scr, carry_scr, *, tm):
    s = pl.program_id(1)
    j = pl.program_id(2)
    lane = lax.broadcasted_iota(I32, (tm, LANES), 1)
    rows = tm // ROW_SPLIT
    lane_rows = lax.broadcasted_iota(I32, (rows, LANES), 1)

    def small_projection():
        small = _dot(h_scr[...], ws_ref[...])

        is_ik = lane < IDX_DIM
        mu = jnp.sum(jnp.where(is_ik, small, 0.0), axis=-1, keepdims=True) * (1.0 / IDX_DIM)
        xc = small - mu
        var = jnp.sum(jnp.where(is_ik, xc * xc, 0.0), axis=-1, keepdims=True) * (1.0 / IDX_DIM)
        y = xc * lax.rsqrt(var + LN_EPS) * par_ref[1:2, :] + par_ref[2:3, :]
        yr = _rope(y, cos_ref[0], sin_ref[0], lane)
        ik_ref[0] = jnp.where(is_ik, yr, pltpu.roll(yr, HALF, 1)).astype(BF16)

        z = small + par_ref[0:1, :]
        ls = jnp.minimum(z, 0.0) - jnp.log1p(jnp.exp(-jnp.abs(z)))
        is_f = (lane >= L_FL) & (lane < L_FL + FOX_HEADS)
        ls = jnp.where(is_f, ls, 0.0)
        is_iw = (lane >= L_IW) & (lane < L_IW + IDX_HEADS)
        other = small * jnp.where(is_iw, float((IDX_HEADS * IDX_DIM) ** -0.5), 1.0)

        @pl.when(s == 0)
        def _():
            carry_scr[...] = jnp.zeros_like(carry_scr)

        blk = LANES
        tri = (lax.broadcasted_iota(I32, (blk, blk), 0) >= lax.broadcasted_iota(I32, (blk, blk), 1))
        tri = jnp.where(tri, 1.0, 0.0).astype(BF16)
        lane_blk = lax.broadcasted_iota(I32, (blk, LANES), 1)
        is_f_blk = (lane_blk >= L_FL) & (lane_blk < L_FL + FOX_HEADS)
        cums = []
        for r in range(tm // blk):
            hi, mid, lo = _split3(ls[r * blk:(r + 1) * blk])
            cums.append(_dot(tri, hi) + _dot(tri, mid) + _dot(tri, lo))
        carry = carry_scr[0:1, :]
        for r in range(tm // blk):
            sm_ref[0, r * blk:(r + 1) * blk, :] = jnp.where(is_f_blk, cums[r] + carry, other[r * blk:(r + 1) * blk])
            carry = carry + cums[r][blk - 1:blk, :]
        carry_scr[0:1, :] = carry

    @pl.when(j == 0)
    def _first_step():
        def post(g, t, tile):
            sl = slice(g * rows, (g + 1) * rows)
            tile = tile * LOG2E if t == T_DQ else tile
            return _rope(tile, cos_ref[0, sl, :], sin_ref[0, sl, :], lane_rows)
        _project_step(h_scr, w_ref, p_ref, post,
                      normalize_rows=lambda g: _rms_rows(x_ref, g_ref, h_scr, g), after_normalize=small_projection)

    @pl.when(j == T_FQ // TENSORS_PER_STEP)
    def _fox_qkv_step():
        _project_step(h_scr, w_ref, p_ref, lambda g, t, tile: tile * LOG2E if t == T_FQ % TENSORS_PER_STEP else tile)

    @pl.when(j == T_FG // TENSORS_PER_STEP)
    def _gates_step():
        gates = (T_FG % TENSORS_PER_STEP, T_DG % TENSORS_PER_STEP)
        _project_step(h_scr, w_ref, p_ref, lambda g, t, tile: tile * jax.nn.sigmoid(tile) if t in gates else tile)


def _proj0(x, g, w_main, w_small, cos_t, sin_t, par, *, tm):
    B, S, D = x.shape
    n_tiles = w_main.shape[1] // WIDTH
    hp = WIDTH // LANES
    per = TENSORS_PER_STEP
    assert n_tiles % per == 0 and (T_DQ, T_DK, T_IQ) == (0, 1, 2) and T_FQ % per == 0
    return pl.pallas_call(
        functools.partial(_proj0_kernel, tm=tm),
        grid=(B, S // tm, n_tiles // per),
        in_specs=[
            pl.BlockSpec((1, tm, D), lambda b, s, j: (b, s, 0)),
            pl.BlockSpec((1, D), lambda b, s, j: (0, 0)),
            pl.BlockSpec((D, per * WIDTH), lambda b, s, j: (0, j)),
            pl.BlockSpec((D, LANES), lambda b, s, j: (0, 0)),
            pl.BlockSpec((1, tm, LANES), lambda b, s, j: (b, s, 0)),
            pl.BlockSpec((1, tm, LANES), lambda b, s, j: (b, s, 0)),
            pl.BlockSpec((8, LANES), lambda b, s, j: (0, 0)),
        ],
        out_specs=[
            pl.BlockSpec((per, 1, hp, tm, LANES), lambda b, s, j: (j, b, 0, s, 0)),
            pl.BlockSpec((1, tm, LANES), lambda b, s, j: (b, s, 0)),
            pl.BlockSpec((1, tm, LANES), lambda b, s, j: (b, s, 0)),
        ],
        out_shape=[
            jax.ShapeDtypeStruct((n_tiles, B, hp, S, LANES), BF16),
            jax.ShapeDtypeStruct((B, S, LANES), BF16),
            jax.ShapeDtypeStruct((B, S, LANES), F32),
        ],
        scratch_shapes=[pltpu.VMEM((tm, D), BF16), pltpu.VMEM((8, LANES), F32)],
        compiler_params=_cparams(3),
        name="proj0",
    )(x, g, w_main, w_small, cos_t, sin_t, par)


def _proj1_kernel(x_ref, g_ref, w_ref, p_ref, h_scr, *, q_tiles):
    j = pl.program_id(2)

    @pl.when(j == 0)
    def _first_step():
        _project_step(h_scr, w_ref, p_ref, lambda g, t, tile: tile * LOG2E if t < q_tiles else tile,
                      normalize_rows=lambda g: _rms_rows(x_ref, g_ref, h_scr, g))

    @pl.when(j > 0)
    def _value_gate_step():
        _project_step(h_scr, w_ref, p_ref, lambda g, t, tile: tile * jax.nn.sigmoid(tile) if t >= q_tiles else tile)


def _proj1(x, g, w, *, tm, n_tensors):
    B, S, D = x.shape
    n_tiles = w.shape[1] // WIDTH
    per = n_tiles // n_tensors
    hp = WIDTH // LANES
    step = 2 * per
    return pl.pallas_call(
        functools.partial(_proj1_kernel, q_tiles=per),
        grid=(B, S // tm, n_tiles // step),
        in_specs=[
            pl.BlockSpec((1, tm, D), lambda b, s, j: (b, s, 0)),
            pl.BlockSpec((1, D), lambda b, s, j: (0, 0)),
            pl.BlockSpec((D, step * WIDTH), lambda b, s, j: (0, j)),
        ],
        out_specs=pl.BlockSpec((step, 1, hp, tm, LANES), lambda b, s, j: (j, b, 0, s, 0)),
        out_shape=jax.ShapeDtypeStruct((n_tiles, B, hp, S, LANES), BF16),
        scratch_shapes=[pltpu.VMEM((tm, D), BF16)],
        compiler_params=_cparams(3),
        name="proj1",
    )(x, g, w)


VT_ROWS = HEAD_DIM + PACKED_ROWS


def _flash_scratch(n_heads, tq):
    return [pltpu.VMEM((n_heads, tq, tq), F32), pltpu.VMEM((n_heads, 1, tq), F32),
            pltpu.VMEM((n_heads, VT_ROWS, tq), F32)]


def _store_values_t(vt_scr, hp, c, v_blk):
    v_t = v_blk.astype(F32).T
    tk = v_t.shape[1]
    ones_row = jnp.where(lax.broadcasted_iota(I32, (PACKED_ROWS, tk), 0) == 0, 1.0, 0.0).astype(BF16)
    for hh in range(2):
        vt_scr[2 * hp + hh, c, 0:HEAD_DIM, :] = v_t[hh * HEAD_DIM:(hh + 1) * HEAD_DIM].astype(BF16)
        vt_scr[2 * hp + hh, c, HEAD_DIM:VT_ROWS, :] = ones_row


def _flash_update(h, s_t, vt, m_scr, acc_scr):
    m = m_scr[h]
    m_new = jnp.maximum(m, jnp.max(s_t, axis=0, keepdims=True))
    alpha = jnp.exp2(m - m_new)
    p = jnp.exp2((s_t - m_new).astype(BF16))
    m_scr[h] = m_new
    acc_scr[h] = alpha * acc_scr[h] + _dot(vt, p)


def _flash_attend(n_heads, lo, hi, scores_fn, values_fn, scratch, last_scores_fixup=None):
    s_scr, m_scr, acc_scr = scratch
    m_scr[...] = jnp.full(m_scr.shape, NEG_INF, F32)
    acc_scr[...] = jnp.zeros(acc_scr.shape, F32)
    for h in range(n_heads):
        s_scr[h] = scores_fn(lo, h)

    def body(kb, _):
        for h in range(n_heads):
            s_t = s_scr[h]
            s_scr[h] = scores_fn(kb + 1, h)
            _flash_update(h, s_t, values_fn(kb, h), m_scr, acc_scr)
        return 0
    lax.fori_loop(lo, hi - 1, body, 0)

    for h in range(n_heads):
        s_t = s_scr[h]
        if last_scores_fixup is not None:
            s_t = last_scores_fixup(s_t)
        _flash_update(h, s_t, values_fn(hi - 1, h), m_scr, acc_scr)


def _finish_pair(hp, gate, acc_scr):
    heads = [acc_scr[2 * hp + hh, 0:HEAD_DIM, :] * (1.0 / acc_scr[2 * hp + hh, HEAD_DIM:HEAD_DIM + 1, :])
             for hh in range(2)]
    return (jnp.concatenate(heads, axis=0).T * gate.astype(F32)).astype(BF16)


def _split_pair(q, lane):
    del lane
    q_t = q.astype(F32).T
    first = lax.broadcasted_iota(I32, q_t.shape, 0) < HALF
    return jnp.where(first, q_t, 0.0).astype(BF16), jnp.where(first, 0.0, q_t).astype(BF16)


def _fox_aug_lanes(f_tile, n_heads, is_key):
    parts = jnp.concatenate(_split3(f_tile * LOG2E), axis=1)
    src = lax.broadcasted_iota(I32, (3 * LANES, n_heads * LANES), 0)
    dst = lax.broadcasted_iota(I32, (3 * LANES, n_heads * LANES), 1)
    head, dst_lane = dst >> LANE_SHIFT, dst & (LANES - 1)
    term, src_lane = src >> LANE_SHIFT, src & (LANES - 1)
    first = L_AUG if is_key else L_AUG + 3
    place = (src_lane == L_FL + head) & (dst_lane == first + term)
    placed = _dot(parts, jnp.where(place, -1.0 if is_key else 1.0, 0.0).astype(BF16))
    lane = lax.broadcasted_iota(I32, placed.shape, 1) & (LANES - 1)
    ones_first = L_AUG + 3 if is_key else L_AUG
    return placed + jnp.where((lane >= ones_first) & (lane < ones_first + 3), 1.0, 0.0)


def _fox_kernel(q_ref, k_ref, v_ref, g_ref, f_ref, o_ref, ka_scr, vt_scr, *flash_scr, tq):
    qi = pl.program_id(1)
    tk = tq
    n_pairs = k_ref.shape[2]
    n_blocks = k_ref.shape[3] // tk
    lane = lax.broadcasted_iota(I32, (tq, LANES), 1)

    def head_tile(pair_tile, aug, h):
        base = pair_tile if h % 2 == 0 else pltpu.roll(pair_tile, HALF, 1)
        return jnp.where(lane < L_AUG, base, aug[:, h * LANES:(h + 1) * LANES])

    @pl.when(qi == 0)
    def _prepare():
        def body(c, _):
            r0 = pl.multiple_of(c * tk, tk)
            aug = _fox_aug_lanes(f_ref[0, pl.ds(r0, tk), :], 2 * n_pairs, True)
            for hp in range(n_pairs):
                kf = k_ref[0, 0, hp, pl.ds(r0, tk), :].astype(F32)
                for h in (2 * hp, 2 * hp + 1):
                    ka_scr[h, pl.ds(r0, tk), :] = head_tile(kf, aug, h).astype(BF16)
                _store_values_t(vt_scr, hp, c, v_ref[0, 0, hp, pl.ds(r0, tk), :])
            return 0
        lax.fori_loop(0, n_blocks, body, 0)

    q0 = pl.multiple_of(qi * tq, tq)
    q_aug = _fox_aug_lanes(f_ref[0, pl.ds(q0, tq), :], 2 * n_pairs, False)
    qa = []
    for hp in range(n_pairs):
        qf = q_ref[0, 0, hp].astype(F32)
        qa.extend(head_tile(qf, q_aug, h).T.astype(BF16) for h in (2 * hp, 2 * hp + 1))

    def scores(kb, h):
        return _dot(ka_scr[h, pl.ds(_block_start(kb, tk), tk), :], qa[h])

    def values(kb, h):
        return vt_scr[h, kb]

    def causal(s_t):
        future = lax.broadcasted_iota(I32, (tk, tq), 0) > lax.broadcasted_iota(I32, (tk, tq), 1)
        return jnp.where(future, NEG_INF, s_t)

    _flash_attend(2 * n_pairs, 0, qi + 1, scores, values, flash_scr, last_scores_fixup=causal)
    for hp in range(n_pairs):
        o_ref[0, hp] = _finish_pair(hp, g_ref[0, 0, hp], flash_scr[2])


def _fox(p0, small, *, tq):
    _, B, HP, S, _ = p0.shape
    spec_q = lambda t: pl.BlockSpec((1, 1, HP, tq, LANES), lambda b, q, t=t: (t, b, 0, q, 0))
    spec_s = lambda t: pl.BlockSpec((1, 1, HP, S, LANES), lambda b, q, t=t: (t, b, 0, 0, 0),
                                    pipeline_mode=pl.Buffered(1))
    return pl.pallas_call(
        functools.partial(_fox_kernel, tq=tq),
        grid=(B, S // tq),
        in_specs=[spec_q(T_FQ), spec_s(T_FK), spec_s(T_FV), spec_q(T_FG),
                  pl.BlockSpec((1, S, LANES), lambda b, q: (b, 0, 0), pipeline_mode=pl.Buffered(1))],
        out_specs=pl.BlockSpec((1, HP, tq, LANES), lambda b, q: (b, 0, q, 0)),
        out_shape=jax.ShapeDtypeStruct((B, HP, S, LANES), BF16),
        scratch_shapes=[pltpu.VMEM((2 * HP, S, LANES), BF16), pltpu.VMEM((2 * HP, S // tq, VT_ROWS, tq), BF16)]
        + _flash_scratch(2 * HP, tq),
        compiler_params=_cparams(2),
        name="fox_attn",
    )(p0, p0, p0, p0, small)


def _dsa_kernel(q_ref, k_ref, v_ref, iq_ref, g_ref, ik_ref, sm_ref, o_ref, vt_scr, key_scr, hi_scr, lo_scr,
                *flash_scr, tq, top_k):
    qi = pl.program_id(1)
    tk = tq
    n_pairs = k_ref.shape[2]
    seq = k_ref.shape[3]
    n_blocks = seq // tk
    lane = lax.broadcasted_iota(I32, (tq, LANES), 1)

    @pl.when(qi == 0)
    def _prepare():
        def body(c, _):
            r0 = pl.multiple_of(c * tk, tk)
            for hp in range(n_pairs):
                _store_values_t(vt_scr, hp, c, v_ref[0, 0, hp, pl.ds(r0, tk), :])
            return 0
        lax.fori_loop(0, n_blocks, body, 0)

    iw_t = sm_ref[0].T[L_IW:L_IW + IDX_HEADS, :]
    iq_m = []
    for hp in range(IDX_HEADS // 2):
        iq_m.extend(_split_pair(iq_ref[0, 0, hp], lane))

    def index_keys(r0):
        ikb = ik_ref[0, pl.ds(r0, tk), :]
        score = jnp.zeros((tk, tq), F32)
        for h in range(IDX_HEADS):
            score = score + jnp.maximum(_dot(ikb, iq_m[h]), 0.0) * iw_t[h:h + 1, :]
        bits = pltpu.bitcast(score, I32)
        return bits ^ ((bits >> 31) & 0x7FFFFFFF)

    def store_keys(r0, key):
        key_scr[pl.ds(r0, tk), :] = key
        hi_scr[pl.ds(r0, tk), :] = (key >> HALF_BITS).astype(I16)
        lo_scr[pl.ds(r0, tk), :] = ((key & HALF_MASK) + I16_MIN).astype(I16)

    def fill(kb, _):
        r0 = pl.multiple_of(kb * tk, tk)
        store_keys(r0, index_keys(r0))
        return 0
    lax.fori_loop(0, qi, fill, 0)

    d0 = pl.multiple_of(qi * tk, tk)
    row = lax.broadcasted_iota(I32, (tk, tq), 0)
    col = lax.broadcasted_iota(I32, (tk, tq), 1)
    beyond_chunk = (row >> CHUNK_SHIFT) > (col >> CHUNK_SHIFT)
    store_keys(d0, jnp.where(beyond_chunk, INT_MIN, index_keys(d0)))
    pad0 = pl.multiple_of((qi + 1) * tk, tk)
    hi_scr[pl.ds(pad0, tk), :] = jnp.full((tk, tq), I16_MIN, I16)
    lo_scr[pl.ds(pad0, tk), :] = jnp.full((tk, tq), I16_MIN, I16)

    q_pos = qi * tq + lax.broadcasted_iota(I32, (1, tq), 1)
    n_beyond = (seq - ((q_pos >> CHUNK_SHIFT) + 1) * CHUNK).astype(F32)
    neg_hi = KEY_OF_NEG_INF >> HALF_BITS
    neg_lo = (KEY_OF_NEG_INF & HALF_MASK) + I16_MIN

    rows16 = 2 * tk // PACKED_ROWS

    def count16(ref, pred, t_s):
        t16 = t_s.astype(I16)

        def body(i, accs):
            r0 = pl.multiple_of(i * 2 * tk, 2 * tk)
            one = jnp.where(pred(ref[pl.ds(r0, 2 * tk), :], t16), jnp.int16(1), jnp.int16(0))
            accs = list(accs)
            for g in range(rows16):
                accs[g % len(accs)] = accs[g % len(accs)] + one[g * PACKED_ROWS:(g + 1) * PACKED_ROWS]
            return tuple(accs)
        zero = jnp.zeros((PACKED_ROWS, tq), I16)
        accs = lax.fori_loop(0, (qi + 2) // 2, body, (zero,) * COUNT_CHAINS)
        return jnp.sum(sum(accs[1:], accs[0]).astype(F32), axis=0, keepdims=True)

    def search16(ref, need, beyond, beyond_key):
        def bit_step(i, t_u):
            cand_u = t_u | lax.shift_left(jnp.int32(1), HALF_BITS - 1 - i)
            cand_s = cand_u + I16_MIN
            c = count16(ref, lambda k, t: k >= t, cand_s) + jnp.where(cand_s <= beyond_key, beyond, 0.0)
            return jnp.where(c >= need, cand_u, t_u)
        return lax.fori_loop(0, HALF_BITS, bit_step, jnp.zeros((1, tq), I32)) + I16_MIN

    thr_hi = search16(hi_scr, float(top_k), n_beyond, neg_hi)
    n_above_hi = count16(hi_scr, lambda k, t: k > t, thr_hi) + jnp.where(thr_hi < neg_hi, n_beyond, 0.0)

    thr_hi16 = thr_hi.astype(I16)

    def keep_low(i, _):
        r0 = pl.multiple_of(i * 2 * tk, 2 * tk)
        sl = pl.ds(r0, 2 * tk)
        lo_scr[sl, :] = jnp.where(hi_scr[sl, :] == thr_hi16, lo_scr[sl, :], jnp.int16(I16_MIN))
        return 0
    lax.fori_loop(0, (qi + 2) // 2, keep_low, 0)

    beyond_lo = jnp.where(thr_hi == neg_hi, n_beyond, 0.0)
    thr_lo = search16(lo_scr, top_k - n_above_hi, beyond_lo, neg_lo)
    n_above_lo = count16(lo_scr, lambda k, t: k > t, thr_lo)
    n_ties = count16(lo_scr, lambda k, t: k >= t, thr_lo) - n_above_lo
    n_above = n_above_hi + n_above_lo + jnp.where(thr_lo < neg_lo, beyond_lo, 0.0)
    thr = lax.shift_left(thr_hi, HALF_BITS) | (thr_lo - I16_MIN)
    n_ties_kept = top_k - n_above
    must_rank_ties = jnp.max(jnp.where(n_ties > n_ties_kept, 1.0, 0.0)) > 0.5

    @pl.when(must_rank_ties)
    def _mask_with_tie_ranks():
        strict_lower = jnp.where(row > col, 1.0, 0.0).astype(BF16)

        def make_bias(kb, ties_before):
            r0 = pl.multiple_of(kb * tk, tk)
            kblk = key_scr[pl.ds(r0, tk), :]
            eq = kblk == thr
            eq_f = jnp.where(eq, 1.0, 0.0)
            rank = jnp.dot(strict_lower, eq_f.astype(BF16), preferred_element_type=F32) + ties_before
            bias = jnp.where(kblk > thr, 0.0, jnp.where(eq, jnp.where(rank < n_ties_kept, 0.0, NEG_INF), NEG_INF))
            key_scr[pl.ds(r0, tk), :] = pltpu.bitcast(bias, I32)
            return ties_before + jnp.sum(eq_f, axis=0, keepdims=True)
        lax.fori_loop(0, qi + 1, make_bias, jnp.zeros((1, tq), F32))

    @pl.when(jnp.logical_not(must_rank_ties))
    def _mask_all_ties_kept():
        def make_bias(kb, _):
            r0 = pl.multiple_of(kb * tk, tk)
            bias = jnp.where(key_scr[pl.ds(r0, tk), :] >= thr, 0.0, NEG_INF)
            key_scr[pl.ds(r0, tk), :] = pltpu.bitcast(bias, I32)
            return 0
        lax.fori_loop(0, qi + 1, make_bias, 0)

    qm = []
    for hp in range(n_pairs):
        qm.extend(_split_pair(q_ref[0, 0, hp], lane))

    def scores(kb, h):
        r0 = _block_start(kb, tk)
        bias = pltpu.bitcast(key_scr[pl.ds(r0, tk), :], F32)
        return _dot(k_ref[0, 0, h // 2, pl.ds(r0, tk), :], qm[h]) + bias

    def values(kb, h):
        return vt_scr[h, kb]

    _flash_attend(2 * n_pairs, 0, qi + 1, scores, values, flash_scr)
    for hp in range(n_pairs):
        o_ref[0, hp] = _finish_pair(hp, g_ref[0, 0, hp], flash_scr[2])


def _dsa(p0, ik2, small, *, tq, top_k):
    _, B, HP, S, _ = p0.shape
    spec_q = lambda t: pl.BlockSpec((1, 1, HP, tq, LANES), lambda b, q, t=t: (t, b, 0, q, 0))
    spec_s = lambda t: pl.BlockSpec((1, 1, HP, S, LANES), lambda b, q, t=t: (t, b, 0, 0, 0),
                                    pipeline_mode=pl.Buffered(1))
    return pl.pallas_call(
        functools.partial(_dsa_kernel, tq=tq, top_k=top_k),
        grid=(B, S // tq),
        in_specs=[spec_q(T_DQ), spec_s(T_DK), spec_s(T_DV), spec_q(T_IQ), spec_q(T_DG),
                  pl.BlockSpec((1, S, LANES), lambda b, q: (b, 0, 0), pipeline_mode=pl.Buffered(1)),
                  pl.BlockSpec((1, tq, LANES), lambda b, q: (b, q, 0))],
        out_specs=pl.BlockSpec((1, HP, tq, LANES), lambda b, q: (b, 0, q, 0)),
        out_shape=jax.ShapeDtypeStruct((B, HP, S, LANES), BF16),
        scratch_shapes=[pltpu.VMEM((2 * HP, S // tq, VT_ROWS, tq), BF16), pltpu.VMEM((S, tq), I32),
                        pltpu.VMEM((S + tq, tq), I16), pltpu.VMEM((S + tq, tq), I16)]
        + _flash_scratch(2 * HP, tq),
        compiler_params=_cparams(2),
        name="dsa_attn",
    )(p0, p0, p0, p0, p0, ik2, small)


def _chunk_kernel(q_ref, k_ref, v_ref, g_ref, bt_ref, o_ref, vt_scr, *flash_scr, tq):
    qi = pl.program_id(2)
    tk = tq
    n_pairs = k_ref.shape[2]
    n_blocks = k_ref.shape[3] // tk
    band_blocks = bt_ref.shape[1] // tk
    lane = lax.broadcasted_iota(I32, (tq, LANES), 1)

    @pl.when(qi == 0)
    def _prepare():
        def body(c, _):
            r0 = pl.multiple_of(c * tk, tk)
            for hp in range(n_pairs):
                _store_values_t(vt_scr, hp, c, v_ref[0, 0, hp, pl.ds(r0, tk), :])
            return 0
        lax.fori_loop(0, n_blocks, body, 0)

    qm = []
    for hp in range(n_pairs):
        qm.extend(_split_pair(q_ref[0, 0, hp], lane))

    def scores(jb, h):
        r0 = _block_start(qi - (band_blocks - 1) + jb, tk)
        b0 = _block_start(jb, tk)
        return _dot(k_ref[0, 0, h // 2, pl.ds(r0, tk), :], qm[h]) + bt_ref[h, pl.ds(b0, tk), :]

    def values(jb, h):
        return vt_scr[h, qi - (band_blocks - 1) + jb]

    first = jnp.maximum(band_blocks - 1 - qi, 0)
    _flash_attend(2 * n_pairs, first, band_blocks, scores, values, flash_scr)
    for hp in range(n_pairs):
        o_ref[0, hp] = _finish_pair(hp, g_ref[0, 0, hp], flash_scr[2])


def _chunk_attn(p1, bias_t, *, tq, n_tensors):
    n_tiles, B, pp, S, _ = p1.shape
    groups = n_tiles // n_tensors
    HP = groups * pp
    band = bias_t.shape[1]
    spec_q = lambda t: pl.BlockSpec((1, 1, pp, tq, LANES), lambda b, h, q, t=t: (t * groups + h, b, 0, q, 0))
    spec_s = lambda t: pl.BlockSpec((1, 1, pp, S, LANES), lambda b, h, q, t=t: (t * groups + h, b, 0, 0, 0),
                                    pipeline_mode=pl.Buffered(1))
    return pl.pallas_call(
        functools.partial(_chunk_kernel, tq=tq),
        grid=(B, groups, S // tq),
        in_specs=[spec_q(0), spec_s(1), spec_s(2), spec_q(3),
                  pl.BlockSpec((2 * pp, band, tq), lambda b, h, q: (h, 0, 0), pipeline_mode=pl.Buffered(1))],
        out_specs=pl.BlockSpec((1, pp, tq, LANES), lambda b, h, q: (b, h, q, 0)),
        out_shape=jax.ShapeDtypeStruct((B, HP, S, LANES), BF16),
        scratch_shapes=[pltpu.VMEM((2 * pp, S // tq, VT_ROWS, tq), BF16)] + _flash_scratch(2 * pp, tq),
        compiler_params=_cparams(3),
        name="chunk_attn",
    )(p1, p1, p1, p1, bias_t)


def _band_bias_t(rel_table, *, tq):
    left = CHK_LEFT_CHUNKS * CHUNK
    pad = -(-left // tq) * tq
    rows = pad + tq
    r = jnp.arange(rows)[:, None]
    c = jnp.arange(tq)[None, :]
    back = (c + pad) // CHUNK - r // CHUNK
    in_band = (back >= 0) & (back <= CHK_LEFT_CHUNKS)
    period = rows + tq
    j = jnp.arange(period)
    c_minus_r = jnp.where(j < tq, j, j - period)
    bucket = jnp.clip(c_minus_r + pad, -MAX_REL_DIST, MAX_REL_DIST) + MAX_REL_DIST
    per_offset = rel_table.astype(F32)[:, bucket]
    n_heads = rel_table.shape[0]
    bias = jnp.tile(per_offset, (1, rows))[:, :rows * (period - 1)].reshape(n_heads, rows, period - 1)[:, :, :tq]
    return jnp.where(in_band[None], bias * LOG2E, NEG_INF)


def _out_kernel(*refs, n_in):
    a_refs = refs[:n_in]
    w_ref, x_ref, g_ref, o_ref = refs[n_in:]
    parts = []
    for a in a_refs:
        parts.extend(a[0, hp] for hp in range(a.shape[1]))
    att = jnp.concatenate(parts, axis=-1)
    y = jnp.dot(att, w_ref[...], preferred_element_type=F32)
    ms = jnp.mean(y * y, axis=-1, keepdims=True)
    o_ref[0] = x_ref[0] + (y * lax.rsqrt(ms + RMS_EPS)) * g_ref[...]


def _out_proj(atts, w, x, g, *, tm):
    B, S, D = x.shape
    n_in = len(atts)
    a_specs = [pl.BlockSpec((1, a.shape[1], tm, LANES), lambda b, s: (b, 0, s, 0)) for a in atts]
    return pl.pallas_call(
        functools.partial(_out_kernel, n_in=n_in),
        grid=(B, S // tm),
        in_specs=a_specs + [
            pl.BlockSpec(w.shape, lambda b, s: (0, 0)),
            pl.BlockSpec((1, tm, D), lambda b, s: (b, s, 0)),
            pl.BlockSpec((1, D), lambda b, s: (0, 0)),
        ],
        out_specs=pl.BlockSpec((1, tm, D), lambda b, s: (b, s, 0)),
        out_shape=jax.ShapeDtypeStruct((B, S, D), F32),
        compiler_params=_cparams(2),
        name=f"out_proj{n_in}",
    )(*atts, w, x, g)


def _rope_tiles(positions):
    inv_freq = ROPE_THETA ** (-jnp.arange(0, ROT_DIM, 2, dtype=F32) / ROT_DIM)
    ang = positions.astype(F32)[..., None] * inv_freq
    cos, sin = jnp.cos(ang), jnp.sin(ang)
    half = ROT_DIM // 2
    one = jnp.ones(ang.shape[:-1] + (HEAD_DIM - ROT_DIM,), F32)
    cos_head = jnp.concatenate([cos, cos, one], axis=-1)
    sin_head = jnp.concatenate([-sin, sin, 0.0 * one], axis=-1)
    assert cos_head.shape[-1] == HEAD_DIM and half * 2 == ROT_DIM
    return jnp.tile(cos_head, (1, 1, 2)), jnp.tile(sin_head, (1, 1, 2))


def _layer0_weights(w_in, b_forget, idx_k_g, idx_k_b):
    scale = HEAD_DIM ** -0.5
    off = 0
    cols = {}
    for name, n in (("fq", WIDTH), ("fk", WIDTH), ("fv", WIDTH), ("fl", FOX_HEADS), ("fg", WIDTH),
                    ("dq", WIDTH), ("dk", WIDTH), ("dv", WIDTH), ("iq", WIDTH), ("ik", IDX_DIM),
                    ("iw", IDX_HEADS), ("dg", WIDTH)):
        cols[name] = w_in[:, off:off + n]
        off += n
    assert off == w_in.shape[1]
    order = [None] * 9
    order[T_DQ], order[T_DK], order[T_IQ] = cols["dq"] * scale, cols["dk"], cols["iq"]
    order[T_FQ], order[T_FK], order[T_FV], order[T_FG] = cols["fq"] * scale, cols["fk"], cols["fv"], cols["fg"]
    order[T_DV], order[T_DG] = cols["dv"], cols["dg"]
    w_main = jnp.concatenate(order, axis=1).astype(BF16)
    pad = jnp.zeros((w_in.shape[0], LANES - L_IW - IDX_HEADS), w_in.dtype)
    w_small = jnp.concatenate([cols["ik"], cols["fl"], cols["iw"], pad], axis=1).astype(BF16)
    par = jnp.zeros((8, LANES), F32)
    par = par.at[0, L_FL:L_FL + FOX_HEADS].set(b_forget.astype(F32))
    par = par.at[1, :IDX_DIM].set(idx_k_g.astype(F32))
    par = par.at[2, :IDX_DIM].set(idx_k_b.astype(F32))
    return w_main, w_small, par


def _even_layer(x, cos_t, sin_t, pre_g, post_g, w_in, b_forget, idx_k_g, idx_k_b, w_out, top_k, cfg):
    w_main, w_small, par = _layer0_weights(w_in, b_forget, idx_k_g, idx_k_b)
    p0, ik2, small = _proj0(x, pre_g[None].astype(F32), w_main, w_small, cos_t, sin_t, par, tm=cfg["tm_proj"])
    fox = _fox(p0, small, tq=cfg["tq"])
    dsa = _dsa(p0, ik2, small, tq=cfg["tq"], top_k=top_k)
    return _out_proj([fox, dsa], w_out.astype(BF16), x, post_g[None].astype(F32), tm=cfg["tm_out"])


def _odd_layer(x, pre_g, post_g, w_in, rel_table, w_out, cfg):
    scale = HEAD_DIM ** -0.5
    width = CHK_HEADS * HEAD_DIM
    w = jnp.concatenate([w_in[:, :width] * scale, w_in[:, width:]], axis=1).astype(BF16)
    p1 = _proj1(x, pre_g[None].astype(F32), w, tm=cfg["tm_proj"], n_tensors=4)
    att = _chunk_attn(p1, _band_bias_t(rel_table, tq=cfg["tq"]), tq=cfg["tq"], n_tensors=4)
    return _out_proj([att], w_out.astype(BF16), x, post_g[None].astype(F32), tm=cfg["tm_out"])


def _config(seq):
    tq = 256
    assert seq % tq == 0 and tq % CHUNK == 0
    tm_proj = min(1024, seq)
    tm_out = min(512, seq)
    assert seq % tm_proj == 0 and seq % tm_out == 0
    return {"tq": tq, "tm_proj": tm_proj, "tm_out": tm_out}


def kernel(x, positions, pre_norm_g, post_norm_g, w_in_even, b_forget, idx_k_g, idx_k_b,
           w_out_even, w_in_odd, rel_bias, w_out_odd):
    seq = x.shape[1]
    depth = pre_norm_g.shape[0]
    cfg = _config(seq)
    top_k = min(DSA_TOPK, seq // 4)
    cos_t, sin_t = _rope_tiles(positions)
    for layer in range(depth):
        i = layer // 2
        if layer % 2 == 0:
            x = _even_layer(x, cos_t, sin_t, pre_norm_g[layer], post_norm_g[layer], w_in_even[i],
                            b_forget[i], idx_k_g[i], idx_k_b[i], w_out_even[i], top_k, cfg)
        else:
            x = _odd_layer(x, pre_norm_g[layer], post_norm_g[layer], w_in_odd[i], rel_bias[i], w_out_odd[i], cfg)
    return x
```

```python
import functools

import numpy as np
import jax
import jax.numpy as jnp
from jax import lax
from jax.experimental import pallas as pl
from jax.experimental.pallas import tpu as pltpu

F32 = jnp.float32
BF16 = jnp.bfloat16
I32 = jnp.int32

LANES = 128
LANE_SHIFT = 7
MXU_COLS = 256
HEAD_DIM = 64
HALF = HEAD_DIM
ROT_DIM = HEAD_DIM // 4
ROPE_THETA = 500000.0
RMS_EPS = 1e-6
LN_EPS = 1e-6
NEG_INF = -1e30
LOG2E = 1.4426950408889634
CHUNK = 64
CHUNK_SHIFT = 6
FOX_HEADS = 8
DSA_HEADS = 8
IDX_HEADS = 8
IDX_DIM = 64
DSA_TOPK = 256
CHK_HEADS = 16
CHK_LEFT_CHUNKS = 8
MAX_REL_DIST = 128

WIDTH = 512
T_DQ, T_DK, T_IQ, T_FQ, T_FK, T_FV, T_FG, T_DV, T_DG = range(9)
TENSORS_PER_STEP = 3
L_IK = 0
L_FL = 64
L_IW = 72
L_AUG = 64

VMEM_LIMIT = 52 * 1024 * 1024

INT_MIN = -2147483648
I16 = jnp.int16
I16_MIN = -32768
HALF_BITS = 16
HALF_MASK = 0xFFFF
PACKED_ROWS = 16
COUNT_CHAINS = 4


def _sortable_key_of(value):
    b = int(np.array(value, np.float32).view(np.int32))
    return b ^ ((b >> 31) & 0x7FFFFFFF)


KEY_OF_NEG_INF = _sortable_key_of(NEG_INF)


def _cparams(n_axes):
    return pltpu.CompilerParams(dimension_semantics=("arbitrary",) * n_axes,
                                vmem_limit_bytes=VMEM_LIMIT)


def _dot(a, b):
    return jnp.dot(a, b, preferred_element_type=F32)


def _block_start(index, size):
    return index * size if isinstance(index, int) else pl.multiple_of(index * size, size)


def _rope(t, cos_t, sin_t, lane):
    first = (lane & (HALF - 1)) < (ROT_DIM // 2)
    partner = jnp.where(first, pltpu.roll(t, LANES - ROT_DIM // 2, 1), pltpu.roll(t, ROT_DIM // 2, 1))
    return t * cos_t + partner * sin_t


def _split3(x):
    hi = x.astype(BF16)
    r1 = x - hi.astype(F32)
    mid = r1.astype(BF16)
    lo = (r1 - mid.astype(F32)).astype(BF16)
    return hi, mid, lo


ROW_SPLIT = 2


def _project_step(h_scr, w_ref, p_ref, post, normalize_rows=None, after_normalize=None):
    tm = h_scr.shape[0]
    rows = tm // ROW_SPLIT
    n_chunks = w_ref.shape[1] // MXU_COLS
    per_chunk = MXU_COLS // LANES
    per_tensor = WIDTH // LANES

    def emit(g, c, r):
        for k in range(per_chunk):
            pair = c * per_chunk + k
            t, hp = pair // per_tensor, pair % per_tensor
            tile = post(g, t, r[:, k * LANES:(k + 1) * LANES])
            p_ref[t, 0, hp, g * rows:(g + 1) * rows, :] = tile.astype(BF16)

    if normalize_rows is not None:
        normalize_rows(0)
    pending = None
    for i, (g, c) in enumerate((g, c) for g in range(ROW_SPLIT) for c in range(n_chunks)):
        r = _dot(h_scr[g * rows:(g + 1) * rows, :], w_ref[:, c * MXU_COLS:(c + 1) * MXU_COLS])
        if i == 0 and normalize_rows is not None:
            for later in range(1, ROW_SPLIT):
                normalize_rows(later)
            if after_normalize is not None:
                after_normalize()
        if pending is not None:
            emit(*pending)
        pending = (g, c, r)
    emit(*pending)


def _rms_rows(x_ref, g_ref, h_scr, g):
    rows = h_scr.shape[0] // ROW_SPLIT
    xf = x_ref[0, g * rows:(g + 1) * rows, :]
    ms = jnp.mean(xf * xf, axis=-1, keepdims=True)
    h_scr[g * rows:(g + 1) * rows, :] = ((xf * lax.rsqrt(ms + RMS_EPS)) * g_ref[...]).astype(BF16)


def _proj0_kernel(x_ref, g_ref, w_ref, ws_ref, cos_ref, sin_ref, par_ref,
                  p_ref, ik_ref, sm_ref, h_scr, carry_scr, *, tm):
    s = pl.program_id(1)
    j = pl.program_id(2)
    lane = lax.broadcasted_iota(I32, (tm, LANES), 1)
    rows = tm // ROW_SPLIT
    lane_rows = lax.broadcasted_iota(I32, (rows, LANES), 1)

    def small_projection():
        small = _dot(h_scr[...], ws_ref[...])

        is_ik = lane < IDX_DIM
        mu = jnp.sum(jnp.where(is_ik, small, 0.0), axis=-1, keepdims=True) * (1.0 / IDX_DIM)
        xc = small - mu
        var = jnp.sum(jnp.where(is_ik, xc * xc, 0.0), axis=-1, keepdims=True) * (1.0 / IDX_DIM)
        y = xc * lax.rsqrt(var + LN_EPS) * par_ref[1:2, :] + par_ref[2:3, :]
        yr = _rope(y, cos_ref[0], sin_ref[0], lane)
        ik_ref[0] = jnp.where(is_ik, yr, pltpu.roll(yr, HALF, 1)).astype(BF16)

        z = small + par_ref[0:1, :]
        ls = jnp.minimum(z, 0.0) - jnp.log1p(jnp.exp(-jnp.abs(z)))
        is_f = (lane >= L_FL) & (lane < L_FL + FOX_HEADS)
        ls = jnp.where(is_f, ls, 0.0)
        is_iw = (lane >= L_IW) & (lane < L_IW + IDX_HEADS)
        other = small * jnp.where(is_iw, float((IDX_HEADS * IDX_DIM) ** -0.5), 1.0)

        @pl.when(s == 0)
        def _():
            carry_scr[...] = jnp.zeros_like(carry_scr)

        blk = LANES
        tri = (lax.broadcasted_iota(I32, (blk, blk), 0) >= lax.broadcasted_iota(I32, (blk, blk), 1))
        tri = jnp.where(tri, 1.0, 0.0).astype(BF16)
        lane_blk = lax.broadcasted_iota(I32, (blk, LANES), 1)
        is_f_blk = (lane_blk >= L_FL) & (lane_blk < L_FL + FOX_HEADS)
        cums = []
        for r in range(tm // blk):
            hi, mid, lo = _split3(ls[r * blk:(r + 1) * blk])
            cums.append(_dot(tri, hi) + _dot(tri, mid) + _dot(tri, lo))
        carry = carry_scr[0:1, :]
        for r in range(tm // blk):
            sm_ref[0, r * blk:(r + 1) * blk, :] = jnp.where(is_f_blk, cums[r] + carry, other[r * blk:(r + 1) * blk])
            carry = carry + cums[r][blk - 1:blk, :]
        carry_scr[0:1, :] = carry

    @pl.when(j == 0)
    def _first_step():
        def post(g, t, tile):
            sl = slice(g * rows, (g + 1) * rows)
            tile = tile * LOG2E if t == T_DQ else tile
            return _rope(tile, cos_ref[0, sl, :], sin_ref[0, sl, :], lane_rows)
        _project_step(h_scr, w_ref, p_ref, post,
                      normalize_rows=lambda g: _rms_rows(x_ref, g_ref, h_scr, g), after_normalize=small_projection)

    @pl.when(j == T_FQ // TENSORS_PER_STEP)
    def _fox_qkv_step():
        _project_step(h_scr, w_ref, p_ref, lambda g, t, tile: tile * LOG2E if t == T_FQ % TENSORS_PER_STEP else tile)

    @pl.when(j == T_FG // TENSORS_PER_STEP)
    def _gates_step():
        gates = (T_FG % TENSORS_PER_STEP, T_DG % TENSORS_PER_STEP)
        _project_step(h_scr, w_ref, p_ref, lambda g, t, tile: tile * jax.nn.sigmoid(tile) if t in gates else tile)


def _proj0(x, g, w_main, w_small, cos_t, sin_t, par, *, tm):
    B, S, D = x.shape
    n_tiles = w_main.shape[1] // WIDTH
    hp = WIDTH // LANES
    per = TENSORS_PER_STEP
    assert n_tiles % per == 0 and (T_DQ, T_DK, T_IQ) == (0, 1, 2) and T_FQ % per == 0
    return pl.pallas_call(
        functools.partial(_proj0_kernel, tm=tm),
        grid=(B, S // tm, n_tiles // per),
        in_specs=[
            pl.BlockSpec((1, tm, D), lambda b, s, j: (b, s, 0)),
            pl.BlockSpec((1, D), lambda b, s, j: (0, 0)),
            pl.BlockSpec((D, per * WIDTH), lambda b, s, j: (0, j)),
            pl.BlockSpec((D, LANES), lambda b, s, j: (0, 0)),
            pl.BlockSpec((1, tm, LANES), lambda b, s, j: (b, s, 0)),
            pl.BlockSpec((1, tm, LANES), lambda b, s, j: (b, s, 0)),
            pl.BlockSpec((8, LANES), lambda b, s, j: (0, 0)),
        ],
        out_specs=[
            pl.BlockSpec((per, 1, hp, tm, LANES), lambda b, s, j: (j, b, 0, s, 0)),
            pl.BlockSpec((1, tm, LANES), lambda b, s, j: (b, s, 0)),
            pl.BlockSpec((1, tm, LANES), lambda b, s, j: (b, s, 0)),
        ],
        out_shape=[
            jax.ShapeDtypeStruct((n_tiles, B, hp, S, LANES), BF16),
            jax.ShapeDtypeStruct((B, S, LANES), BF16),
            jax.ShapeDtypeStruct((B, S, LANES), F32),
        ],
        scratch_shapes=[pltpu.VMEM((tm, D), BF16), pltpu.VMEM((8, LANES), F32)],
        compiler_params=_cparams(3),
        name="proj0",
    )(x, g, w_main, w_small, cos_t, sin_t, par)


def _proj1_kernel(x_ref, g_ref, w_ref, p_ref, h_scr, *, q_tiles):
    j = pl.program_id(2)

    @pl.when(j == 0)
    def _first_step():
        _project_step(h_scr, w_ref, p_ref, lambda g, t, tile: tile * LOG2E if t < q_tiles else tile,
                      normalize_rows=lambda g: _rms_rows(x_ref, g_ref, h_scr, g))

    @pl.when(j > 0)
    def _value_gate_step():
        _project_step(h_scr, w_ref, p_ref, lambda g, t, tile: tile * jax.nn.sigmoid(tile) if t >= q_tiles else tile)


def _proj1(x, g, w, *, tm, n_tensors):
    B, S, D = x.shape
    n_tiles = w.shape[1] // WIDTH
    per = n_tiles // n_tensors
    hp = WIDTH // LANES
    step = 2 * per
    return pl.pallas_call(
        functools.partial(_proj1_kernel, q_tiles=per),
        grid=(B, S // tm, n_tiles // step),
        in_specs=[
            pl.BlockSpec((1, tm, D), lambda b, s, j: (b, s, 0)),
            pl.BlockSpec((1, D), lambda b, s, j: (0, 0)),
            pl.BlockSpec((D, step * WIDTH), lambda b, s, j: (0, j)),
        ],
        out_specs=pl.BlockSpec((step, 1, hp, tm, LANES), lambda b, s, j: (j, b, 0, s, 0)),
        out_shape=jax.ShapeDtypeStruct((n_tiles, B, hp, S, LANES), BF16),
        scratch_shapes=[pltpu.VMEM((tm, D), BF16)],
        compiler_params=_cparams(3),
        name="proj1",
    )(x, g, w)


VT_ROWS = HEAD_DIM + PACKED_ROWS


def _flash_scratch(n_heads, tq):
    return [pltpu.VMEM((n_heads, tq, tq), F32), pltpu.VMEM((n_heads, 1, tq), F32),
            pltpu.VMEM((n_heads, VT_ROWS, tq), F32)]


def _store_values_t(vt_scr, hp, c, v_blk):
    v_t = v_blk.astype(F32).T
    tk = v_t.shape[1]
    ones_row = jnp.where(lax.broadcasted_iota(I32, (PACKED_ROWS, tk), 0) == 0, 1.0, 0.0).astype(BF16)
    for hh in range(2):
        vt_scr[2 * hp + hh, c, 0:HEAD_DIM, :] = v_t[hh * HEAD_DIM:(hh + 1) * HEAD_DIM].astype(BF16)
        vt_scr[2 * hp + hh, c, HEAD_DIM:VT_ROWS, :] = ones_row


def _flash_update(h, s_t, vt, m_scr, acc_scr):
    m = m_scr[h]
    m_new = jnp.maximum(m, jnp.max(s_t, axis=0, keepdims=True))
    alpha = jnp.exp2(m - m_new)
    p = jnp.exp2((s_t - m_new).astype(BF16))
    m_scr[h] = m_new
    acc_scr[h] = alpha * acc_scr[h] + _dot(vt, p)


def _flash_attend(n_heads, lo, hi, scores_fn, values_fn, scratch, last_scores_fixup=None):
    s_scr, m_scr, acc_scr = scratch
    m_scr[...] = jnp.full(m_scr.shape, NEG_INF, F32)
    acc_scr[...] = jnp.zeros(acc_scr.shape, F32)
    for h in range(n_heads):
        s_scr[h] = scores_fn(lo, h)

    def body(kb, _):
        for h in range(n_heads):
            s_t = s_scr[h]
            s_scr[h] = scores_fn(kb + 1, h)
            _flash_update(h, s_t, values_fn(kb, h), m_scr, acc_scr)
        return 0
    lax.fori_loop(lo, hi - 1, body, 0)

    for h in range(n_heads):
        s_t = s_scr[h]
        if last_scores_fixup is not None:
            s_t = last_scores_fixup(s_t)
        _flash_update(h, s_t, values_fn(hi - 1, h), m_scr, acc_scr)


def _finish_pair(hp, gate, acc_scr):
    heads = [acc_scr[2 * hp + hh, 0:HEAD_DIM, :] * (1.0 / acc_scr[2 * hp + hh, HEAD_DIM:HEAD_DIM + 1, :])
             for hh in range(2)]
    return (jnp.concatenate(heads, axis=0).T * gate.astype(F32)).astype(BF16)


def _split_pair(q, lane):
    del lane
    q_t = q.astype(F32).T
    first = lax.broadcasted_iota(I32, q_t.shape, 0) < HALF
    return jnp.where(first, q_t, 0.0).astype(BF16), jnp.where(first, 0.0, q_t).astype(BF16)


def _fox_aug_lanes(f_tile, n_heads, is_key):
    parts = jnp.concatenate(_split3(f_tile * LOG2E), axis=1)
    src = lax.broadcasted_iota(I32, (3 * LANES, n_heads * LANES), 0)
    dst = lax.broadcasted_iota(I32, (3 * LANES, n_heads * LANES), 1)
    head, dst_lane = dst >> LANE_SHIFT, dst & (LANES - 1)
    term, src_lane = src >> LANE_SHIFT, src & (LANES - 1)
    first = L_AUG if is_key else L_AUG + 3
    place = (src_lane == L_FL + head) & (dst_lane == first + term)
    placed = _dot(parts, jnp.where(place, -1.0 if is_key else 1.0, 0.0).astype(BF16))
    lane = lax.broadcasted_iota(I32, placed.shape, 1) & (LANES - 1)
    ones_first = L_AUG + 3 if is_key else L_AUG
    return placed + jnp.where((lane >= ones_first) & (lane < ones_first + 3), 1.0, 0.0)


def _fox_kernel(q_ref, k_ref, v_ref, g_ref, f_ref, o_ref, ka_scr, vt_scr, *flash_scr, tq):
    qi = pl.program_id(1)
    tk = tq
    n_pairs = k_ref.shape[2]
    n_blocks = k_ref.shape[3] // tk
    lane = lax.broadcasted_iota(I32, (tq, LANES), 1)

    def head_tile(pair_tile, aug, h):
        base = pair_tile if h % 2 == 0 else pltpu.roll(pair_tile, HALF, 1)
        return jnp.where(lane < L_AUG, base, aug[:, h * LANES:(h + 1) * LANES])

    @pl.when(qi == 0)
    def _prepare():
        def body(c, _):
            r0 = pl.multiple_of(c * tk, tk)
            aug = _fox_aug_lanes(f_ref[0, pl.ds(r0, tk), :], 2 * n_pairs, True)
            for hp in range(n_pairs):
                kf = k_ref[0, 0, hp, pl.ds(r0, tk), :].astype(F32)
                for h in (2 * hp, 2 * hp + 1):
                    ka_scr[h, pl.ds(r0, tk), :] = head_tile(kf, aug, h).astype(BF16)
                _store_values_t(vt_scr, hp, c, v_ref[0, 0, hp, pl.ds(r0, tk), :])
            return 0
        lax.fori_loop(0, n_blocks, body, 0)

    q0 = pl.multiple_of(qi * tq, tq)
    q_aug = _fox_aug_lanes(f_ref[0, pl.ds(q0, tq), :], 2 * n_pairs, False)
    qa = []
    for hp in range(n_pairs):
        qf = q_ref[0, 0, hp].astype(F32)
        qa.extend(head_tile(qf, q_aug, h).T.astype(BF16) for h in (2 * hp, 2 * hp + 1))

    def scores(kb, h):
        return _dot(ka_scr[h, pl.ds(_block_start(kb, tk), tk), :], qa[h])

    def values(kb, h):
        return vt_scr[h, kb]

    def causal(s_t):
        future = lax.broadcasted_iota(I32, (tk, tq), 0) > lax.broadcasted_iota(I32, (tk, tq), 1)
        return jnp.where(future, NEG_INF, s_t)

    _flash_attend(2 * n_pairs, 0, qi + 1, scores, values, flash_scr, last_scores_fixup=causal)
    for hp in range(n_pairs):
        o_ref[0, hp] = _finish_pair(hp, g_ref[0, 0, hp], flash_scr[2])


def _fox(p0, small, *, tq):
    _, B, HP, S, _ = p0.shape
    spec_q = lambda t: pl.BlockSpec((1, 1, HP, tq, LANES), lambda b, q, t=t: (t, b, 0, q, 0))
    spec_s = lambda t: pl.BlockSpec((1, 1, HP, S, LANES), lambda b, q, t=t: (t, b, 0, 0, 0),
                                    pipeline_mode=pl.Buffered(1))
    return pl.pallas_call(
        functools.partial(_fox_kernel, tq=tq),
        grid=(B, S // tq),
        in_specs=[spec_q(T_FQ), spec_s(T_FK), spec_s(T_FV), spec_q(T_FG),
                  pl.BlockSpec((1, S, LANES), lambda b, q: (b, 0, 0), pipeline_mode=pl.Buffered(1))],
        out_specs=pl.BlockSpec((1, HP, tq, LANES), lambda b, q: (b, 0, q, 0)),
        out_shape=jax.ShapeDtypeStruct((B, HP, S, LANES), BF16),
        scratch_shapes=[pltpu.VMEM((2 * HP, S, LANES), BF16), pltpu.VMEM((2 * HP, S // tq, VT_ROWS, tq), BF16)]
        + _flash_scratch(2 * HP, tq),
        compiler_params=_cparams(2),
        name="fox_attn",
    )(p0, p0, p0, p0, small)


def _dsa_kernel(q_ref, k_ref, v_ref, iq_ref, g_ref, ik_ref, sm_ref, o_ref, vt_scr, key_scr, hi_scr, lo_scr,
                *flash_scr, tq, top_k):
    qi = pl.program_id(1)
    tk = tq
    n_pairs = k_ref.shape[2]
    seq = k_ref.shape[3]
    n_blocks = seq // tk
    lane = lax.broadcasted_iota(I32, (tq, LANES), 1)

    @pl.when(qi == 0)
    def _prepare():
        def body(c, _):
            r0 = pl.multiple_of(c * tk, tk)
            for hp in range(n_pairs):
                _store_values_t(vt_scr, hp, c, v_ref[0, 0, hp, pl.ds(r0, tk), :])
            return 0
        lax.fori_loop(0, n_blocks, body, 0)

    iw_t = sm_ref[0].T[L_IW:L_IW + IDX_HEADS, :]
    iq_m = []
    for hp in range(IDX_HEADS // 2):
        iq_m.extend(_split_pair(iq_ref[0, 0, hp], lane))

    def index_keys(r0):
        ikb = ik_ref[0, pl.ds(r0, tk), :]
        score = jnp.zeros((tk, tq), F32)
        for h in range(IDX_HEADS):
            score = score + jnp.maximum(_dot(ikb, iq_m[h]), 0.0) * iw_t[h:h + 1, :]
        bits = pltpu.bitcast(score, I32)
        return bits ^ ((bits >> 31) & 0x7FFFFFFF)

    def store_keys(r0, key):
        key_scr[pl.ds(r0, tk), :] = key
        hi_scr[pl.ds(r0, tk), :] = (key >> HALF_BITS).astype(I16)
        lo_scr[pl.ds(r0, tk), :] = ((key & HALF_MASK) + I16_MIN).astype(I16)

    def fill(kb):
        r0 = pl.multiple_of(kb * tk, tk)
        store_keys(r0, index_keys(r0))

    def fill_two(i, _):
        fill(2 * i)
        fill(2 * i + 1)
        return 0
    lax.fori_loop(0, qi >> 1, fill_two, 0)

    @pl.when((qi & 1) == 1)
    def _odd_block():
        fill(qi - 1)

    d0 = pl.multiple_of(qi * tk, tk)
    row = lax.broadcasted_iota(I32, (tk, tq), 0)
    col = lax.broadcasted_iota(I32, (tk, tq), 1)
    beyond_chunk = (row >> CHUNK_SHIFT) > (col >> CHUNK_SHIFT)
    store_keys(d0, jnp.where(beyond_chunk, INT_MIN, index_keys(d0)))
    pad0 = pl.multiple_of((qi + 1) * tk, tk)
    hi_scr[pl.ds(pad0, tk), :] = jnp.full((tk, tq), I16_MIN, I16)
    lo_scr[pl.ds(pad0, tk), :] = jnp.full((tk, tq), I16_MIN, I16)

    q_pos = qi * tq + lax.broadcasted_iota(I32, (1, tq), 1)
    n_beyond = (seq - ((q_pos >> CHUNK_SHIFT) + 1) * CHUNK).astype(F32)
    neg_hi = KEY_OF_NEG_INF >> HALF_BITS
    neg_lo = (KEY_OF_NEG_INF & HALF_MASK) + I16_MIN

    rows16 = 2 * tk // PACKED_ROWS

    def count16(ref, pred, t_s):
        t16 = t_s.astype(I16)

        def body(i, accs):
            r0 = pl.multiple_of(i * 2 * tk, 2 * tk)
            one = jnp.where(pred(ref[pl.ds(r0, 2 * tk), :], t16), jnp.int16(1), jnp.int16(0))
            accs = list(accs)
            for g in range(rows16):
                accs[g % len(accs)] = accs[g % len(accs)] + one[g * PACKED_ROWS:(g + 1) * PACKED_ROWS]
            return tuple(accs)
        zero = jnp.zeros((PACKED_ROWS, tq), I16)
        accs = lax.fori_loop(0, (qi + 2) // 2, body, (zero,) * COUNT_CHAINS)
        return jnp.sum(sum(accs[1:], accs[0]).astype(F32), axis=0, keepdims=True)

    def search16(ref, need, beyond, beyond_key):
        def bit_step(i, t_u):
            cand_u = t_u | lax.shift_left(jnp.int32(1), HALF_BITS - 1 - i)
            cand_s = cand_u + I16_MIN
            c = count16(ref, lambda k, t: k >= t, cand_s) + jnp.where(cand_s <= beyond_key, beyond, 0.0)
            return jnp.where(c >= need, cand_u, t_u)
        return lax.fori_loop(0, HALF_BITS, bit_step, jnp.zeros((1, tq), I32)) + I16_MIN

    thr_hi = search16(hi_scr, float(top_k), n_beyond, neg_hi)
    n_above_hi = count16(hi_scr, lambda k, t: k > t, thr_hi) + jnp.where(thr_hi < neg_hi, n_beyond, 0.0)

    thr_hi16 = thr_hi.astype(I16)

    def keep_low(i, _):
        r0 = pl.multiple_of(i * 2 * tk, 2 * tk)
        sl = pl.ds(r0, 2 * tk)
        lo_scr[sl, :] = jnp.where(hi_scr[sl, :] == thr_hi16, lo_scr[sl, :], jnp.int16(I16_MIN))
        return 0
    lax.fori_loop(0, (qi + 2) // 2, keep_low, 0)

    beyond_lo = jnp.where(thr_hi == neg_hi, n_beyond, 0.0)
    thr_lo = search16(lo_scr, top_k - n_above_hi, beyond_lo, neg_lo)
    n_above_lo = count16(lo_scr, lambda k, t: k > t, thr_lo)
    n_ties = count16(lo_scr, lambda k, t: k >= t, thr_lo) - n_above_lo
    n_above = n_above_hi + n_above_lo + jnp.where(thr_lo < neg_lo, beyond_lo, 0.0)
    thr = lax.shift_left(thr_hi, HALF_BITS) | (thr_lo - I16_MIN)
    n_ties_kept = top_k - n_above
    must_rank_ties = jnp.max(jnp.where(n_ties > n_ties_kept, 1.0, 0.0)) > 0.5

    @pl.when(must_rank_ties)
    def _mask_with_tie_ranks():
        strict_lower = jnp.where(row > col, 1.0, 0.0).astype(BF16)

        def make_bias(kb, ties_before):
            r0 = pl.multiple_of(kb * tk, tk)
            kblk = key_scr[pl.ds(r0, tk), :]
            eq = kblk == thr
            eq_f = jnp.where(eq, 1.0, 0.0)
            rank = jnp.dot(strict_lower, eq_f.astype(BF16), preferred_element_type=F32) + ties_before
            bias = jnp.where(kblk > thr, 0.0, jnp.where(eq, jnp.where(rank < n_ties_kept, 0.0, NEG_INF), NEG_INF))
            key_scr[pl.ds(r0, tk), :] = pltpu.bitcast(bias, I32)
            return ties_before + jnp.sum(eq_f, axis=0, keepdims=True)
        lax.fori_loop(0, qi + 1, make_bias, jnp.zeros((1, tq), F32))

    @pl.when(jnp.logical_not(must_rank_ties))
    def _mask_all_ties_kept():
        def make_bias(kb, _):
            r0 = pl.multiple_of(kb * tk, tk)
            bias = jnp.where(key_scr[pl.ds(r0, tk), :] >= thr, 0.0, NEG_INF)
            key_scr[pl.ds(r0, tk), :] = pltpu.bitcast(bias, I32)
            return 0
        lax.fori_loop(0, qi + 1, make_bias, 0)

    qm = []
    for hp in range(n_pairs):
        qm.extend(_split_pair(q_ref[0, 0, hp], lane))

    def scores(kb, h):
        r0 = _block_start(kb, tk)
        bias = pltpu.bitcast(key_scr[pl.ds(r0, tk), :], F32)
        return _dot(k_ref[0, 0, h // 2, pl.ds(r0, tk), :], qm[h]) + bias

    def values(kb, h):
        return vt_scr[h, kb]

    _flash_attend(2 * n_pairs, 0, qi + 1, scores, values, flash_scr)
    for hp in range(n_pairs):
        o_ref[0, hp] = _finish_pair(hp, g_ref[0, 0, hp], flash_scr[2])


def _dsa(p0, ik2, small, *, tq, top_k):
    _, B, HP, S, _ = p0.shape
    spec_q = lambda t: pl.BlockSpec((1, 1, HP, tq, LANES), lambda b, q, t=t: (t, b, 0, q, 0))
    spec_s = lambda t: pl.BlockSpec((1, 1, HP, S, LANES), lambda b, q, t=t: (t, b, 0, 0, 0),
                                    pipeline_mode=pl.Buffered(1))
    return pl.pallas_call(
        functools.partial(_dsa_kernel, tq=tq, top_k=top_k),
        grid=(B, S // tq),
        in_specs=[spec_q(T_DQ), spec_s(T_DK), spec_s(T_DV), spec_q(T_IQ), spec_q(T_DG),
                  pl.BlockSpec((1, S, LANES), lambda b, q: (b, 0, 0), pipeline_mode=pl.Buffered(1)),
                  pl.BlockSpec((1, tq, LANES), lambda b, q: (b, q, 0))],
        out_specs=pl.BlockSpec((1, HP, tq, LANES), lambda b, q: (b, 0, q, 0)),
        out_shape=jax.ShapeDtypeStruct((B, HP, S, LANES), BF16),
        scratch_shapes=[pltpu.VMEM((2 * HP, S // tq, VT_ROWS, tq), BF16), pltpu.VMEM((S, tq), I32),
                        pltpu.VMEM((S + tq, tq), I16), pltpu.VMEM((S + tq, tq), I16)]
        + _flash_scratch(2 * HP, tq),
        compiler_params=_cparams(2),
        name="dsa_attn",
    )(p0, p0, p0, p0, p0, ik2, small)


def _chunk_kernel(q_ref, k_ref, v_ref, g_ref, bt_ref, o_ref, vt_scr, *flash_scr, tq):
    qi = pl.program_id(2)
    tk = tq
    n_pairs = k_ref.shape[2]
    n_blocks = k_ref.shape[3] // tk
    band_blocks = bt_ref.shape[1] // tk
    lane = lax.broadcasted_iota(I32, (tq, LANES), 1)

    @pl.when(qi == 0)
    def _prepare():
        def body(c, _):
            r0 = pl.multiple_of(c * tk, tk)
            for hp in range(n_pairs):
                _store_values_t(vt_scr, hp, c, v_ref[0, 0, hp, pl.ds(r0, tk), :])
            return 0
        lax.fori_loop(0, n_blocks, body, 0)

    qm = []
    for hp in range(n_pairs):
        qm.extend(_split_pair(q_ref[0, 0, hp], lane))

    def scores(jb, h):
        r0 = _block_start(qi - (band_blocks - 1) + jb, tk)
        b0 = _block_start(jb, tk)
        return _dot(k_ref[0, 0, h // 2, pl.ds(r0, tk), :], qm[h]) + bt_ref[h, pl.ds(b0, tk), :]

    def values(jb, h):
        return vt_scr[h, qi - (band_blocks - 1) + jb]

    first = jnp.maximum(band_blocks - 1 - qi, 0)
    _flash_attend(2 * n_pairs, first, band_blocks, scores, values, flash_scr)
    for hp in range(n_pairs):
        o_ref[0, hp] = _finish_pair(hp, g_ref[0, 0, hp], flash_scr[2])


def _chunk_attn(p1, bias_t, *, tq, n_tensors):
    n_tiles, B, pp, S, _ = p1.shape
    groups = n_tiles // n_tensors
    HP = groups * pp
    band = bias_t.shape[1]
    spec_q = lambda t: pl.BlockSpec((1, 1, pp, tq, LANES), lambda b, h, q, t=t: (t * groups + h, b, 0, q, 0))
    spec_s = lambda t: pl.BlockSpec((1, 1, pp, S, LANES), lambda b, h, q, t=t: (t * groups + h, b, 0, 0, 0),
                                    pipeline_mode=pl.Buffered(1))
    return pl.pallas_call(
        functools.partial(_chunk_kernel, tq=tq),
        grid=(B, groups, S // tq),
        in_specs=[spec_q(0), spec_s(1), spec_s(2), spec_q(3),
                  pl.BlockSpec((2 * pp, band, tq), lambda b, h, q: (h, 0, 0), pipeline_mode=pl.Buffered(1))],
        out_specs=pl.BlockSpec((1, pp, tq, LANES), lambda b, h, q: (b, h, q, 0)),
        out_shape=jax.ShapeDtypeStruct((B, HP, S, LANES), BF16),
        scratch_shapes=[pltpu.VMEM((2 * pp, S // tq, VT_ROWS, tq), BF16)] + _flash_scratch(2 * pp, tq),
        compiler_params=_cparams(3),
        name="chunk_attn",
    )(p1, p1, p1, p1, bias_t)


def _band_bias_t(rel_table, *, tq):
    left = CHK_LEFT_CHUNKS * CHUNK
    pad = -(-left // tq) * tq
    rows = pad + tq
    r = jnp.arange(rows)[:, None]
    c = jnp.arange(tq)[None, :]
    back = (c + pad) // CHUNK - r // CHUNK
    in_band = (back >= 0) & (back <= CHK_LEFT_CHUNKS)
    period = rows + tq
    j = jnp.arange(period)
    c_minus_r = jnp.where(j < tq, j, j - period)
    bucket = jnp.clip(c_minus_r + pad, -MAX_REL_DIST, MAX_REL_DIST) + MAX_REL_DIST
    per_offset = rel_table.astype(F32)[:, bucket]
    n_heads = rel_table.shape[0]
    bias = jnp.tile(per_offset, (1, rows))[:, :rows * (period - 1)].reshape(n_heads, rows, period - 1)[:, :, :tq]
    return jnp.where(in_band[None], bias * LOG2E, NEG_INF)


def _out_kernel(*refs, n_in):
    a_refs = refs[:n_in]
    w_ref, x_ref, g_ref, o_ref = refs[n_in:]
    rows = o_ref.shape[1] // ROW_SPLIT
    ys = []
    for g in range(ROW_SPLIT):
        sl = slice(g * rows, (g + 1) * rows)
        parts = []
        for a in a_refs:
            parts.extend(a[0, hp, sl, :] for hp in range(a.shape[1]))
        ys.append(_dot(jnp.concatenate(parts, axis=-1), w_ref[...]))
    for g, y in enumerate(ys):
        sl = slice(g * rows, (g + 1) * rows)
        ms = jnp.mean(y * y, axis=-1, keepdims=True)
        o_ref[0, sl, :] = x_ref[0, sl, :] + (y * lax.rsqrt(ms + RMS_EPS)) * g_ref[...]


def _out_proj(atts, w, x, g, *, tm):
    B, S, D = x.shape
    n_in = len(atts)
    a_specs = [pl.BlockSpec((1, a.shape[1], tm, LANES), lambda b, s: (b, 0, s, 0)) for a in atts]
    return pl.pallas_call(
        functools.partial(_out_kernel, n_in=n_in),
        grid=(B, S // tm),
        in_specs=a_specs + [
            pl.BlockSpec(w.shape, lambda b, s: (0, 0)),
            pl.BlockSpec((1, tm, D), lambda b, s: (b, s, 0)),
            pl.BlockSpec((1, D), lambda b, s: (0, 0)),
        ],
        out_specs=pl.BlockSpec((1, tm, D), lambda b, s: (b, s, 0)),
        out_shape=jax.ShapeDtypeStruct((B, S, D), F32),
        compiler_params=_cparams(2),
        name=f"out_proj{n_in}",
    )(*atts, w, x, g)


def _rope_tiles(positions):
    inv_freq = ROPE_THETA ** (-jnp.arange(0, ROT_DIM, 2, dtype=F32) / ROT_DIM)
    ang = positions.astype(F32)[..., None] * inv_freq
    cos, sin = jnp.cos(ang), jnp.sin(ang)
    half = ROT_DIM // 2
    one = jnp.ones(ang.shape[:-1] + (HEAD_DIM - ROT_DIM,), F32)
    cos_head = jnp.concatenate([cos, cos, one], axis=-1)
    sin_head = jnp.concatenate([-sin, sin, 0.0 * one], axis=-1)
    assert cos_head.shape[-1] == HEAD_DIM and half * 2 == ROT_DIM
    return jnp.tile(cos_head, (1, 1, 2)), jnp.tile(sin_head, (1, 1, 2))


def _layer0_weights(w_in, b_forget, idx_k_g, idx_k_b):
    scale = HEAD_DIM ** -0.5
    off = 0
    cols = {}
    for name, n in (("fq", WIDTH), ("fk", WIDTH), ("fv", WIDTH), ("fl", FOX_HEADS), ("fg", WIDTH),
                    ("dq", WIDTH), ("dk", WIDTH), ("dv", WIDTH), ("iq", WIDTH), ("ik", IDX_DIM),
                    ("iw", IDX_HEADS), ("dg", WIDTH)):
        cols[name] = w_in[:, off:off + n]
        off += n
    assert off == w_in.shape[1]
    order = [None] * 9
    order[T_DQ], order[T_DK], order[T_IQ] = cols["dq"] * scale, cols["dk"], cols["iq"]
    order[T_FQ], order[T_FK], order[T_FV], order[T_FG] = cols["fq"] * scale, cols["fk"], cols["fv"], cols["fg"]
    order[T_DV], order[T_DG] = cols["dv"], cols["dg"]
    w_main = jnp.concatenate(order, axis=1).astype(BF16)
    pad = jnp.zeros((w_in.shape[0], LANES - L_IW - IDX_HEADS), w_in.dtype)
    w_small = jnp.concatenate([cols["ik"], cols["fl"], cols["iw"], pad], axis=1).astype(BF16)
    par = jnp.zeros((8, LANES), F32)
    par = par.at[0, L_FL:L_FL + FOX_HEADS].set(b_forget.astype(F32))
    par = par.at[1, :IDX_DIM].set(idx_k_g.astype(F32))
    par = par.at[2, :IDX_DIM].set(idx_k_b.astype(F32))
    return w_main, w_small, par


def _even_layer(x, cos_t, sin_t, pre_g, post_g, w_in, b_forget, idx_k_g, idx_k_b, w_out, top_k, cfg):
    w_main, w_small, par = _layer0_weights(w_in, b_forget, idx_k_g, idx_k_b)
    p0, ik2, small = _proj0(x, pre_g[None].astype(F32), w_main, w_small, cos_t, sin_t, par, tm=cfg["tm_proj"])
    fox = _fox(p0, small, tq=cfg["tq"])
    dsa = _dsa(p0, ik2, small, tq=cfg["tq"], top_k=top_k)
    return _out_proj([fox, dsa], w_out.astype(BF16), x, post_g[None].astype(F32), tm=cfg["tm_out"])


def _odd_layer(x, pre_g, post_g, w_in, rel_table, w_out, cfg):
    scale = HEAD_DIM ** -0.5
    width = CHK_HEADS * HEAD_DIM
    w = jnp.concatenate([w_in[:, :width] * scale, w_in[:, width:]], axis=1).astype(BF16)
    p1 = _proj1(x, pre_g[None].astype(F32), w, tm=cfg["tm_proj"], n_tensors=4)
    att = _chunk_attn(p1, _band_bias_t(rel_table, tq=cfg["tq"]), tq=cfg["tq"], n_tensors=4)
    return _out_proj([att], w_out.astype(BF16), x, post_g[None].astype(F32), tm=cfg["tm_out"])


def _config(seq):
    tq = 256
    assert seq % tq == 0 and tq % CHUNK == 0
    tm_proj = min(1024, seq)
    tm_out = min(512, seq)
    assert seq % tm_proj == 0 and seq % tm_out == 0
    return {"tq": tq, "tm_proj": tm_proj, "tm_out": tm_out}


def kernel(x, positions, pre_norm_g, post_norm_g, w_in_even, b_forget, idx_k_g, idx_k_b,
           w_out_even, w_in_odd, rel_bias, w_out_odd):
    seq = x.shape[1]
    depth = pre_norm_g.shape[0]
    cfg = _config(seq)
    top_k = min(DSA_TOPK, seq // 4)
    cos_t, sin_t = _rope_tiles(positions)
    for layer in range(depth):
        i = layer // 2
        if layer % 2 == 0:
            x = _even_layer(x, cos_t, sin_t, pre_norm_g[layer], post_norm_g[layer], w_in_even[i],
                            b_forget[i], idx_k_g[i], idx_k_b[i], w_out_even[i], top_k, cfg)
        else:
            x = _odd_layer(x, pre_norm_g[layer], post_norm_g[layer], w_in_odd[i], rel_bias[i], w_out_odd[i], cfg)
    return x
```

```python
import functools

import numpy as np
import jax
import jax.numpy as jnp
from jax import lax
from jax.experimental import pallas as pl
from jax.experimental.pallas import tpu as pltpu

F32 = jnp.float32
BF16 = jnp.bfloat16
I32 = jnp.int32

LANES = 128
LANE_SHIFT = 7
MXU_COLS = 256
HEAD_DIM = 64
HALF = HEAD_DIM
ROT_DIM = HEAD_DIM // 4
ROPE_THETA = 500000.0
RMS_EPS = 1e-6
LN_EPS = 1e-6
NEG_INF = -1e30
LOG2E = 1.4426950408889634
CHUNK = 64
CHUNK_SHIFT = 6
FOX_HEADS = 8
DSA_HEADS = 8
IDX_HEADS = 8
IDX_DIM = 64
DSA_TOPK = 256
CHK_HEADS = 16
CHK_LEFT_CHUNKS = 8
MAX_REL_DIST = 128

WIDTH = 512
T_DQ, T_DK, T_IQ, T_FQ, T_FK, T_FV, T_FG, T_DV, T_DG = range(9)
TENSORS_PER_STEP = 3
L_IK = 0
L_FL = 64
L_IW = 72
L_AUG = 64

VMEM_LIMIT = 52 * 1024 * 1024

INT_MIN = -2147483648
I16 = jnp.int16
I16_MIN = -32768
HALF_BITS = 16
HALF_MASK = 0xFFFF
PACKED_ROWS = 16
COUNT_CHAINS = 4


def _sortable_key_of(value):
    b = int(np.array(value, np.float32).view(np.int32))
    return b ^ ((b >> 31) & 0x7FFFFFFF)


KEY_OF_NEG_INF = _sortable_key_of(NEG_INF)


def _cparams(n_axes):
    return pltpu.CompilerParams(dimension_semantics=("arbitrary",) * n_axes,
                                vmem_limit_bytes=VMEM_LIMIT)


def _dot(a, b):
    return jnp.dot(a, b, preferred_element_type=F32)


def _block_start(index, size):
    return index * size if isinstance(index, int) else pl.multiple_of(index * size, size)


def _rope(t, cos_t, sin_t, lane):
    first = (lane & (HALF - 1)) < (ROT_DIM // 2)
    partner = jnp.where(first, pltpu.roll(t, LANES - ROT_DIM // 2, 1), pltpu.roll(t, ROT_DIM // 2, 1))
    return t * cos_t + partner * sin_t


def _split3(x):
    hi = x.astype(BF16)
    r1 = x - hi.astype(F32)
    mid = r1.astype(BF16)
    lo = (r1 - mid.astype(F32)).astype(BF16)
    return hi, mid, lo


ROW_SPLIT = 2


def _project_step(h_scr, w_ref, p_ref, post, normalize_rows=None, after_normalize=None):
    tm = h_scr.shape[0]
    rows = tm // ROW_SPLIT
    n_chunks = w_ref.shape[1] // MXU_COLS
    per_chunk = MXU_COLS // LANES
    per_tensor = WIDTH // LANES

    def emit(g, c, r):
        for k in range(per_chunk):
            pair = c * per_chunk + k
            t, hp = pair // per_tensor, pair % per_tensor
            tile = post(g, t, r[:, k * LANES:(k + 1) * LANES])
            p_ref[t, 0, hp, g * rows:(g + 1) * rows, :] = tile.astype(BF16)

    if normalize_rows is not None:
        normalize_rows(0)
    pending = None
    for i, (g, c) in enumerate((g, c) for g in range(ROW_SPLIT) for c in range(n_chunks)):
        r = _dot(h_scr[g * rows:(g + 1) * rows, :], w_ref[:, c * MXU_COLS:(c + 1) * MXU_COLS])
        if i == 0 and normalize_rows is not None:
            for later in range(1, ROW_SPLIT):
                normalize_rows(later)
            if after_normalize is not None:
                after_normalize()
        if pending is not None:
            emit(*pending)
        pending = (g, c, r)
    emit(*pending)


def _rms_rows(x_ref, g_ref, h_scr, g):
    rows = h_scr.shape[0] // ROW_SPLIT
    xf = x_ref[0, g * rows:(g + 1) * rows, :]
    ms = jnp.mean(xf * xf, axis=-1, keepdims=True)
    h_scr[g * rows:(g + 1) * rows, :] = ((xf * lax.rsqrt(ms + RMS_EPS)) * g_ref[...]).astype(BF16)


def _proj0_kernel(x_ref, g_ref, w_ref, ws_ref, cos_ref, sin_ref, par_ref,
                  p_ref, ik_ref, sm_ref, h_scr, carry_scr, *, tm):
    s = pl.program_id(1)
    j = pl.program_id(2)
    lane = lax.broadcasted_iota(I32, (tm, LANES), 1)
    rows = tm // ROW_SPLIT
    lane_rows = lax.broadcasted_iota(I32, (rows, LANES), 1)

    def small_projection():
        small = _dot(h_scr[...], ws_ref[...])

        is_ik = lane < IDX_DIM
        mu = jnp.sum(jnp.where(is_ik, small, 0.0), axis=-1, keepdims=True) * (1.0 / IDX_DIM)
        xc = small - mu
        var = jnp.sum(jnp.where(is_ik, xc * xc, 0.0), axis=-1, keepdims=True) * (1.0 / IDX_DIM)
        y = xc * lax.rsqrt(var + LN_EPS) * par_ref[1:2, :] + par_ref[2:3, :]
        yr = _rope(y, cos_ref[0], sin_ref[0], lane)
        ik_ref[0] = jnp.where(is_ik, yr, pltpu.roll(yr, HALF, 1)).astype(BF16)

        z = small + par_ref[0:1, :]
        ls = jnp.minimum(z, 0.0) - jnp.log1p(jnp.exp(-jnp.abs(z)))
        is_f = (lane >= L_FL) & (lane < L_FL + FOX_HEADS)
        ls = jnp.where(is_f, ls, 0.0)
        is_iw = (lane >= L_IW) & (lane < L_IW + IDX_HEADS)
        other = small * jnp.where(is_iw, float((IDX_HEADS * IDX_DIM) ** -0.5), 1.0)

        @pl.when(s == 0)
        def _():
            carry_scr[...] = jnp.zeros_like(carry_scr)

        blk = LANES
        tri = (lax.broadcasted_iota(I32, (blk, blk), 0) >= lax.broadcasted_iota(I32, (blk, blk), 1))
        tri = jnp.where(tri, 1.0, 0.0).astype(BF16)
        lane_blk = lax.broadcasted_iota(I32, (blk, LANES), 1)
        is_f_blk = (lane_blk >= L_FL) & (lane_blk < L_FL + FOX_HEADS)
        cums = []
        for r in range(tm // blk):
            hi, mid, lo = _split3(ls[r * blk:(r + 1) * blk])
            cums.append(_dot(tri, hi) + _dot(tri, mid) + _dot(tri, lo))
        carry = carry_scr[0:1, :]
        for r in range(tm // blk):
            sm_ref[0, r * blk:(r + 1) * blk, :] = jnp.where(is_f_blk, cums[r] + carry, other[r * blk:(r + 1) * blk])
            carry = carry + cums[r][blk - 1:blk, :]
        carry_scr[0:1, :] = carry

    @pl.when(j == 0)
    def _first_step():
        def post(g, t, tile):
            sl = slice(g * rows, (g + 1) * rows)
            tile = tile * LOG2E if t == T_DQ else tile
            return _rope(tile, cos_ref[0, sl, :], sin_ref[0, sl, :], lane_rows)
        _project_step(h_scr, w_ref, p_ref, post,
                      normalize_rows=lambda g: _rms_rows(x_ref, g_ref, h_scr, g), after_normalize=small_projection)

    @pl.when(j == T_FQ // TENSORS_PER_STEP)
    def _fox_qkv_step():
        _project_step(h_scr, w_ref, p_ref, lambda g, t, tile: tile * LOG2E if t == T_FQ % TENSORS_PER_STEP else tile)

    @pl.when(j == T_FG // TENSORS_PER_STEP)
    def _gates_step():
        gates = (T_FG % TENSORS_PER_STEP, T_DG % TENSORS_PER_STEP)
        _project_step(h_scr, w_ref, p_ref, lambda g, t, tile: tile * jax.nn.sigmoid(tile) if t in gates else tile)


def _proj0(x, g, w_main, w_small, cos_t, sin_t, par, *, tm):
    B, S, D = x.shape
    n_tiles = w_main.shape[1] // WIDTH
    hp = WIDTH // LANES
    per = TENSORS_PER_STEP
    assert n_tiles % per == 0 and (T_DQ, T_DK, T_IQ) == (0, 1, 2) and T_FQ % per == 0
    return pl.pallas_call(
        functools.partial(_proj0_kernel, tm=tm),
        grid=(B, S // tm, n_tiles // per),
        in_specs=[
            pl.BlockSpec((1, tm, D), lambda b, s, j: (b, s, 0)),
            pl.BlockSpec((1, D), lambda b, s, j: (0, 0)),
            pl.BlockSpec((D, per * WIDTH), lambda b, s, j: (0, j)),
            pl.BlockSpec((D, LANES), lambda b, s, j: (0, 0)),
            pl.BlockSpec((1, tm, LANES), lambda b, s, j: (b, s, 0)),
            pl.BlockSpec((1, tm, LANES), lambda b, s, j: (b, s, 0)),
            pl.BlockSpec((8, LANES), lambda b, s, j: (0, 0)),
        ],
        out_specs=[
            pl.BlockSpec((per, 1, hp, tm, LANES), lambda b, s, j: (j, b, 0, s, 0)),
            pl.BlockSpec((1, tm, LANES), lambda b, s, j: (b, s, 0)),
            pl.BlockSpec((1, tm, LANES), lambda b, s, j: (b, s, 0)),
        ],
        out_shape=[
            jax.ShapeDtypeStruct((n_tiles, B, hp, S, LANES), BF16),
            jax.ShapeDtypeStruct((B, S, LANES), BF16),
            jax.ShapeDtypeStruct((B, S, LANES), F32),
        ],
        scratch_shapes=[pltpu.VMEM((tm, D), BF16), pltpu.VMEM((8, LANES), F32)],
        compiler_params=_cparams(3),
        name="proj0",
    )(x, g, w_main, w_small, cos_t, sin_t, par)


def _proj1_kernel(x_ref, g_ref, w_ref, p_ref, h_scr, *, q_tiles):
    j = pl.program_id(2)

    @pl.when(j == 0)
    def _first_step():
        _project_step(h_scr, w_ref, p_ref, lambda g, t, tile: tile * LOG2E if t < q_tiles else tile,
                      normalize_rows=lambda g: _rms_rows(x_ref, g_ref, h_scr, g))

    @pl.when(j > 0)
    def _value_gate_step():
        _project_step(h_scr, w_ref, p_ref, lambda g, t, tile: tile * jax.nn.sigmoid(tile) if t >= q_tiles else tile)


def _proj1(x, g, w, *, tm, n_tensors):
    B, S, D = x.shape
    n_tiles = w.shape[1] // WIDTH
    per = n_tiles // n_tensors
    hp = WIDTH // LANES
    step = 2 * per
    return pl.pallas_call(
        functools.partial(_proj1_kernel, q_tiles=per),
        grid=(B, S // tm, n_tiles // step),
        in_specs=[
            pl.BlockSpec((1, tm, D), lambda b, s, j: (b, s, 0)),
            pl.BlockSpec((1, D), lambda b, s, j: (0, 0)),
            pl.BlockSpec((D, step * WIDTH), lambda b, s, j: (0, j)),
        ],
        out_specs=pl.BlockSpec((step, 1, hp, tm, LANES), lambda b, s, j: (j, b, 0, s, 0)),
        out_shape=jax.ShapeDtypeStruct((n_tiles, B, hp, S, LANES), BF16),
        scratch_shapes=[pltpu.VMEM((tm, D), BF16)],
        compiler_params=_cparams(3),
        name="proj1",
    )(x, g, w)


VT_ROWS = HEAD_DIM + PACKED_ROWS


def _flash_scratch(n_heads, tq):
    return [pltpu.VMEM((n_heads, tq, tq), F32), pltpu.VMEM((n_heads, 1, tq), F32),
            pltpu.VMEM((n_heads, VT_ROWS, tq), F32)]


def _store_values_t(vt_scr, hp, c, v_blk):
    v_t = v_blk.astype(F32).T
    tk = v_t.shape[1]
    ones_row = jnp.where(lax.broadcasted_iota(I32, (PACKED_ROWS, tk), 0) == 0, 1.0, 0.0).astype(BF16)
    for hh in range(2):
        vt_scr[2 * hp + hh, c, 0:HEAD_DIM, :] = v_t[hh * HEAD_DIM:(hh + 1) * HEAD_DIM].astype(BF16)
        vt_scr[2 * hp + hh, c, HEAD_DIM:VT_ROWS, :] = ones_row


def _flash_update(h, s_t, vt, m_scr, acc_scr):
    m = m_scr[h]
    m_new = jnp.maximum(m, jnp.max(s_t, axis=0, keepdims=True))
    alpha = jnp.exp2(m - m_new)
    p = jnp.exp2((s_t - m_new).astype(BF16))
    m_scr[h] = m_new
    acc_scr[h] = alpha * acc_scr[h] + _dot(vt, p)


def _flash_attend(n_heads, lo, hi, scores_fn, values_fn, scratch, last_scores_fixup=None):
    s_scr, m_scr, acc_scr = scratch
    m_scr[...] = jnp.full(m_scr.shape, NEG_INF, F32)
    acc_scr[...] = jnp.zeros(acc_scr.shape, F32)
    for h in range(n_heads):
        s_scr[h] = scores_fn(lo, h)

    def body(kb, _):
        for h in range(n_heads):
            s_t = s_scr[h]
            s_scr[h] = scores_fn(kb + 1, h)
            _flash_update(h, s_t, values_fn(kb, h), m_scr, acc_scr)
        return 0
    lax.fori_loop(lo, hi - 1, body, 0)

    for h in range(n_heads):
        s_t = s_scr[h]
        if last_scores_fixup is not None:
            s_t = last_scores_fixup(s_t)
        _flash_update(h, s_t, values_fn(hi - 1, h), m_scr, acc_scr)


def _finish_pair(hp, gate, acc_scr):
    heads = [acc_scr[2 * hp + hh, 0:HEAD_DIM, :] * (1.0 / acc_scr[2 * hp + hh, HEAD_DIM:HEAD_DIM + 1, :])
             for hh in range(2)]
    return (jnp.concatenate(heads, axis=0).T * gate.astype(F32)).astype(BF16)


def _split_pair(q, lane):
    del lane
    q_t = q.astype(F32).T
    first = lax.broadcasted_iota(I32, q_t.shape, 0) < HALF
    return jnp.where(first, q_t, 0.0).astype(BF16), jnp.where(first, 0.0, q_t).astype(BF16)


def _fox_aug_lanes(f_tile, n_heads, is_key):
    parts = jnp.concatenate(_split3(f_tile * LOG2E), axis=1)
    src = lax.broadcasted_iota(I32, (3 * LANES, n_heads * LANES), 0)
    dst = lax.broadcasted_iota(I32, (3 * LANES, n_heads * LANES), 1)
    head, dst_lane = dst >> LANE_SHIFT, dst & (LANES - 1)
    term, src_lane = src >> LANE_SHIFT, src & (LANES - 1)
    first = L_AUG if is_key else L_AUG + 3
    place = (src_lane == L_FL + head) & (dst_lane == first + term)
    placed = _dot(parts, jnp.where(place, -1.0 if is_key else 1.0, 0.0).astype(BF16))
    lane = lax.broadcasted_iota(I32, placed.shape, 1) & (LANES - 1)
    ones_first = L_AUG + 3 if is_key else L_AUG
    return placed + jnp.where((lane >= ones_first) & (lane < ones_first + 3), 1.0, 0.0)


def _fox_kernel(q_ref, k_ref, v_ref, g_ref, f_ref, o_ref, ka_scr, vt_scr, *flash_scr, tq):
    qi = pl.program_id(1)
    tk = tq
    n_pairs = k_ref.shape[2]
    n_blocks = k_ref.shape[3] // tk
    lane = lax.broadcasted_iota(I32, (tq, LANES), 1)

    def head_tile(pair_tile, aug, h):
        base = pair_tile if h % 2 == 0 else pltpu.roll(pair_tile, HALF, 1)
        return jnp.where(lane < L_AUG, base, aug[:, h * LANES:(h + 1) * LANES])

    @pl.when(qi == 0)
    def _prepare():
        def body(c, _):
            r0 = pl.multiple_of(c * tk, tk)
            aug = _fox_aug_lanes(f_ref[0, pl.ds(r0, tk), :], 2 * n_pairs, True)
            for hp in range(n_pairs):
                kf = k_ref[0, 0, hp, pl.ds(r0, tk), :].astype(F32)
                for h in (2 * hp, 2 * hp + 1):
                    ka_scr[h, pl.ds(r0, tk), :] = head_tile(kf, aug, h).astype(BF16)
                _store_values_t(vt_scr, hp, c, v_ref[0, 0, hp, pl.ds(r0, tk), :])
            return 0
        lax.fori_loop(0, n_blocks, body, 0)

    q0 = pl.multiple_of(qi * tq, tq)
    q_aug = _fox_aug_lanes(f_ref[0, pl.ds(q0, tq), :], 2 * n_pairs, False)
    qa = []
    for hp in range(n_pairs):
        qf = q_ref[0, 0, hp].astype(F32)
        qa.extend(head_tile(qf, q_aug, h).T.astype(BF16) for h in (2 * hp, 2 * hp + 1))

    def scores(kb, h):
        return _dot(ka_scr[h, pl.ds(_block_start(kb, tk), tk), :], qa[h])

    def values(kb, h):
        return vt_scr[h, kb]

    def causal(s_t):
        future = lax.broadcasted_iota(I32, (tk, tq), 0) > lax.broadcasted_iota(I32, (tk, tq), 1)
        return jnp.where(future, NEG_INF, s_t)

    _flash_attend(2 * n_pairs, 0, qi + 1, scores, values, flash_scr, last_scores_fixup=causal)
    for hp in range(n_pairs):
        o_ref[0, hp] = _finish_pair(hp, g_ref[0, 0, hp], flash_scr[2])


def _fox(p0, small, *, tq):
    _, B, HP, S, _ = p0.shape
    spec_q = lambda t: pl.BlockSpec((1, 1, HP, tq, LANES), lambda b, q, t=t: (t, b, 0, q, 0))
    spec_s = lambda t: pl.BlockSpec((1, 1, HP, S, LANES), lambda b, q, t=t: (t, b, 0, 0, 0),
                                    pipeline_mode=pl.Buffered(1))
    return pl.pallas_call(
        functools.partial(_fox_kernel, tq=tq),
        grid=(B, S // tq),
        in_specs=[spec_q(T_FQ), spec_s(T_FK), spec_s(T_FV), spec_q(T_FG),
                  pl.BlockSpec((1, S, LANES), lambda b, q: (b, 0, 0), pipeline_mode=pl.Buffered(1))],
        out_specs=pl.BlockSpec((1, HP, tq, LANES), lambda b, q: (b, 0, q, 0)),
        out_shape=jax.ShapeDtypeStruct((B, HP, S, LANES), BF16),
        scratch_shapes=[pltpu.VMEM((2 * HP, S, LANES), BF16), pltpu.VMEM((2 * HP, S // tq, VT_ROWS, tq), BF16)]
        + _flash_scratch(2 * HP, tq),
        compiler_params=_cparams(2),
        name="fox_attn",
    )(p0, p0, p0, p0, small)


def _dsa_kernel(q_ref, k_ref, v_ref, iq_ref, g_ref, ik_ref, sm_ref, o_ref, vt_scr, key_scr, hi_scr, lo_scr,
                *flash_scr, tq, top_k):
    qi = pl.program_id(1)
    tk = tq
    n_pairs = k_ref.shape[2]
    seq = k_ref.shape[3]
    n_blocks = seq // tk
    lane = lax.broadcasted_iota(I32, (tq, LANES), 1)

    @pl.when(qi == 0)
    def _prepare():
        def body(c, _):
            r0 = pl.multiple_of(c * tk, tk)
            for hp in range(n_pairs):
                _store_values_t(vt_scr, hp, c, v_ref[0, 0, hp, pl.ds(r0, tk), :])
            return 0
        lax.fori_loop(0, n_blocks, body, 0)

    iw_t = sm_ref[0].T[L_IW:L_IW + IDX_HEADS, :]
    iq_m = []
    for hp in range(IDX_HEADS // 2):
        iq_m.extend(_split_pair(iq_ref[0, 0, hp], lane))

    def index_keys(r0):
        ikb = ik_ref[0, pl.ds(r0, tk), :]
        score = jnp.zeros((tk, tq), F32)
        for h in range(IDX_HEADS):
            score = score + jnp.maximum(_dot(ikb, iq_m[h]), 0.0) * iw_t[h:h + 1, :]
        bits = pltpu.bitcast(score, I32)
        return bits ^ ((bits >> 31) & 0x7FFFFFFF)

    def store_keys(r0, key):
        key_scr[pl.ds(r0, tk), :] = key
        hi_scr[pl.ds(r0, tk), :] = (key >> HALF_BITS).astype(I16)
        lo_scr[pl.ds(r0, tk), :] = ((key & HALF_MASK) + I16_MIN).astype(I16)

    def fill(kb):
        r0 = pl.multiple_of(kb * tk, tk)
        store_keys(r0, index_keys(r0))

    def fill_two(i, _):
        fill(2 * i)
        fill(2 * i + 1)
        return 0
    lax.fori_loop(0, qi >> 1, fill_two, 0)

    @pl.when((qi & 1) == 1)
    def _odd_block():
        fill(qi - 1)

    d0 = pl.multiple_of(qi * tk, tk)
    row = lax.broadcasted_iota(I32, (tk, tq), 0)
    col = lax.broadcasted_iota(I32, (tk, tq), 1)
    beyond_chunk = (row >> CHUNK_SHIFT) > (col >> CHUNK_SHIFT)
    store_keys(d0, jnp.where(beyond_chunk, INT_MIN, index_keys(d0)))
    pad0 = pl.multiple_of((qi + 1) * tk, tk)
    hi_scr[pl.ds(pad0, tk), :] = jnp.full((tk, tq), I16_MIN, I16)
    lo_scr[pl.ds(pad0, tk), :] = jnp.full((tk, tq), I16_MIN, I16)

    q_pos = qi * tq + lax.broadcasted_iota(I32, (1, tq), 1)
    n_beyond = (seq - ((q_pos >> CHUNK_SHIFT) + 1) * CHUNK).astype(F32)
    neg_hi = KEY_OF_NEG_INF >> HALF_BITS
    neg_lo = (KEY_OF_NEG_INF & HALF_MASK) + I16_MIN

    rows16 = 2 * tk // PACKED_ROWS

    def count16(ref, pred, t_s):
        t16 = t_s.astype(I16)

        def body(i, accs):
            r0 = pl.multiple_of(i * 2 * tk, 2 * tk)
            one = jnp.where(pred(ref[pl.ds(r0, 2 * tk), :], t16), jnp.int16(1), jnp.int16(0))
            accs = list(accs)
            for g in range(rows16):
                accs[g % len(accs)] = accs[g % len(accs)] + one[g * PACKED_ROWS:(g + 1) * PACKED_ROWS]
            return tuple(accs)
        zero = jnp.zeros((PACKED_ROWS, tq), I16)
        accs = lax.fori_loop(0, (qi + 2) // 2, body, (zero,) * COUNT_CHAINS)
        return jnp.sum(sum(accs[1:], accs[0]).astype(F32), axis=0, keepdims=True)

    def search16(ref, need, beyond, beyond_key):
        def bit_step(i, t_u):
            cand_u = t_u | lax.shift_left(jnp.int32(1), HALF_BITS - 1 - i)
            cand_s = cand_u + I16_MIN
            c = count16(ref, lambda k, t: k >= t, cand_s) + jnp.where(cand_s <= beyond_key, beyond, 0.0)
            return jnp.where(c >= need, cand_u, t_u)
        return lax.fori_loop(0, HALF_BITS, bit_step, jnp.zeros((1, tq), I32)) + I16_MIN

    thr_hi = search16(hi_scr, float(top_k), n_beyond, neg_hi)
    n_above_hi = count16(hi_scr, lambda k, t: k > t, thr_hi) + jnp.where(thr_hi < neg_hi, n_beyond, 0.0)

    thr_hi16 = thr_hi.astype(I16)

    def keep_low(i, _):
        r0 = pl.multiple_of(i * 2 * tk, 2 * tk)
        sl = pl.ds(r0, 2 * tk)
        lo_scr[sl, :] = jnp.where(hi_scr[sl, :] == thr_hi16, lo_scr[sl, :], jnp.int16(I16_MIN))
        return 0
    lax.fori_loop(0, (qi + 2) // 2, keep_low, 0)

    beyond_lo = jnp.where(thr_hi == neg_hi, n_beyond, 0.0)
    thr_lo = search16(lo_scr, top_k - n_above_hi, beyond_lo, neg_lo)
    n_above_lo = count16(lo_scr, lambda k, t: k > t, thr_lo)
    n_ties = count16(lo_scr, lambda k, t: k >= t, thr_lo) - n_above_lo
    n_above = n_above_hi + n_above_lo + jnp.where(thr_lo < neg_lo, beyond_lo, 0.0)
    thr = lax.shift_left(thr_hi, HALF_BITS) | (thr_lo - I16_MIN)
    n_ties_kept = top_k - n_above
    must_rank_ties = jnp.max(jnp.where(n_ties > n_ties_kept, 1.0, 0.0)) > 0.5

    @pl.when(must_rank_ties)
    def _mask_with_tie_ranks():
        strict_lower = jnp.where(row > col, 1.0, 0.0).astype(BF16)

        def make_bias(kb, ties_before):
            r0 = pl.multiple_of(kb * tk, tk)
            kblk = key_scr[pl.ds(r0, tk), :]
            eq = kblk == thr
            eq_f = jnp.where(eq, 1.0, 0.0)
            rank = jnp.dot(strict_lower, eq_f.astype(BF16), preferred_element_type=F32) + ties_before
            bias = jnp.where(kblk > thr, 0.0, jnp.where(eq, jnp.where(rank < n_ties_kept, 0.0, NEG_INF), NEG_INF))
            key_scr[pl.ds(r0, tk), :] = pltpu.bitcast(bias, I32)
            return ties_before + jnp.sum(eq_f, axis=0, keepdims=True)
        lax.fori_loop(0, qi + 1, make_bias, jnp.zeros((1, tq), F32))

    @pl.when(jnp.logical_not(must_rank_ties))
    def _mask_all_ties_kept():
        def make_bias(kb, _):
            r0 = pl.multiple_of(kb * tk, tk)
            bias = jnp.where(key_scr[pl.ds(r0, tk), :] >= thr, 0.0, NEG_INF)
            key_scr[pl.ds(r0, tk), :] = pltpu.bitcast(bias, I32)
            return 0
        lax.fori_loop(0, qi + 1, make_bias, 0)

    qm = []
    for hp in range(n_pairs):
        qm.extend(_split_pair(q_ref[0, 0, hp], lane))

    def scores(kb, h):
        r0 = _block_start(kb, tk)
        bias = pltpu.bitcast(key_scr[pl.ds(r0, tk), :], F32)
        return _dot(k_ref[0, 0, h // 2, pl.ds(r0, tk), :], qm[h]) + bias

    def values(kb, h):
        return vt_scr[h, kb]

    _flash_attend(2 * n_pairs, 0, qi + 1, scores, values, flash_scr)
    for hp in range(n_pairs):
        o_ref[0, hp] = _finish_pair(hp, g_ref[0, 0, hp], flash_scr[2])


def _dsa(p0, ik2, small, *, tq, top_k):
    _, B, HP, S, _ = p0.shape
    spec_q = lambda t: pl.BlockSpec((1, 1, HP, tq, LANES), lambda b, q, t=t: (t, b, 0, q, 0))
    spec_s = lambda t: pl.BlockSpec((1, 1, HP, S, LANES), lambda b, q, t=t: (t, b, 0, 0, 0),
                                    pipeline_mode=pl.Buffered(1))
    return pl.pallas_call(
        functools.partial(_dsa_kernel, tq=tq, top_k=top_k),
        grid=(B, S // tq),
        in_specs=[spec_q(T_DQ), spec_s(T_DK), spec_s(T_DV), spec_q(T_IQ), spec_q(T_DG),
                  pl.BlockSpec((1, S, LANES), lambda b, q: (b, 0, 0), pipeline_mode=pl.Buffered(1)),
                  pl.BlockSpec((1, tq, LANES), lambda b, q: (b, q, 0))],
        out_specs=pl.BlockSpec((1, HP, tq, LANES), lambda b, q: (b, 0, q, 0)),
        out_shape=jax.ShapeDtypeStruct((B, HP, S, LANES), BF16),
        scratch_shapes=[pltpu.VMEM((2 * HP, S // tq, VT_ROWS, tq), BF16), pltpu.VMEM((S, tq), I32),
                        pltpu.VMEM((S + tq, tq), I16), pltpu.VMEM((S + tq, tq), I16)]
        + _flash_scratch(2 * HP, tq),
        compiler_params=_cparams(2),
        name="dsa_attn",
    )(p0, p0, p0, p0, p0, ik2, small)


def _chunk_kernel(q_ref, k_ref, v_ref, g_ref, bt_ref, o_ref, vt_scr, *flash_scr, tq):
    qi = pl.program_id(2)
    tk = tq
    n_pairs = k_ref.shape[2]
    n_blocks = k_ref.shape[3] // tk
    band_blocks = bt_ref.shape[1] // tk
    lane = lax.broadcasted_iota(I32, (tq, LANES), 1)

    @pl.when(qi == 0)
    def _prepare():
        def body(c, _):
            r0 = pl.multiple_of(c * tk, tk)
            for hp in range(n_pairs):
                _store_values_t(vt_scr, hp, c, v_ref[0, 0, hp, pl.ds(r0, tk), :])
            return 0
        lax.fori_loop(0, n_blocks, body, 0)

    qm = []
    for hp in range(n_pairs):
        qm.extend(_split_pair(q_ref[0, 0, hp], lane))

    def scores(jb, h):
        r0 = _block_start(qi - (band_blocks - 1) + jb, tk)
        b0 = _block_start(jb, tk)
        return _dot(k_ref[0, 0, h // 2, pl.ds(r0, tk), :], qm[h]) + bt_ref[h, pl.ds(b0, tk), :]

    def values(jb, h):
        return vt_scr[h, qi - (band_blocks - 1) + jb]

    first = jnp.maximum(band_blocks - 1 - qi, 0)
    _flash_attend(2 * n_pairs, first, band_blocks, scores, values, flash_scr)
    for hp in range(n_pairs):
        o_ref[0, hp] = _finish_pair(hp, g_ref[0, 0, hp], flash_scr[2])


def _chunk_attn(p1, bias_t, *, tq, n_tensors):
    n_tiles, B, pp, S, _ = p1.shape
    groups = n_tiles // n_tensors
    HP = groups * pp
    band = bias_t.shape[1]
    spec_q = lambda t: pl.BlockSpec((1, 1, pp, tq, LANES), lambda b, h, q, t=t: (t * groups + h, b, 0, q, 0))
    spec_s = lambda t: pl.BlockSpec((1, 1, pp, S, LANES), lambda b, h, q, t=t: (t * groups + h, b, 0, 0, 0),
                                    pipeline_mode=pl.Buffered(1))
    return pl.pallas_call(
        functools.partial(_chunk_kernel, tq=tq),
        grid=(B, groups, S // tq),
        in_specs=[spec_q(0), spec_s(1), spec_s(2), spec_q(3),
                  pl.BlockSpec((2 * pp, band, tq), lambda b, h, q: (h, 0, 0), pipeline_mode=pl.Buffered(1))],
        out_specs=pl.BlockSpec((1, pp, tq, LANES), lambda b, h, q: (b, h, q, 0)),
        out_shape=jax.ShapeDtypeStruct((B, HP, S, LANES), BF16),
        scratch_shapes=[pltpu.VMEM((2 * pp, S // tq, VT_ROWS, tq), BF16)] + _flash_scratch(2 * pp, tq),
        compiler_params=_cparams(3),
        name="chunk_attn",
    )(p1, p1, p1, p1, bias_t)


def _band_bias_t(rel_table, *, tq):
    left = CHK_LEFT_CHUNKS * CHUNK
    pad = -(-left // tq) * tq
    rows = pad + tq
    r = jnp.arange(rows)[:, None]
    c = jnp.arange(tq)[None, :]
    back = (c + pad) // CHUNK - r // CHUNK
    in_band = (back >= 0) & (back <= CHK_LEFT_CHUNKS)
    period = rows + tq
    j = jnp.arange(period)
    c_minus_r = jnp.where(j < tq, j, j - period)
    bucket = jnp.clip(c_minus_r + pad, -MAX_REL_DIST, MAX_REL_DIST) + MAX_REL_DIST
    per_offset = rel_table.astype(F32)[:, bucket]
    n_heads = rel_table.shape[0]
    bias = jnp.tile(per_offset, (1, rows))[:, :rows * (period - 1)].reshape(n_heads, rows, period - 1)[:, :, :tq]
    return jnp.where(in_band[None], bias * LOG2E, NEG_INF)


def _out_kernel(*refs, n_in):
    a_refs = refs[:n_in]
    w_ref, x_ref, g_ref, o_ref = refs[n_in:]
    rows = o_ref.shape[1] // ROW_SPLIT
    ys = []
    for g in range(ROW_SPLIT):
        sl = slice(g * rows, (g + 1) * rows)
        parts = []
        for a in a_refs:
            parts.extend(a[0, hp, sl, :] for hp in range(a.shape[1]))
        ys.append(_dot(jnp.concatenate(parts, axis=-1), w_ref[...]))
    for g, y in enumerate(ys):
        sl = slice(g * rows, (g + 1) * rows)
        ms = jnp.mean(y * y, axis=-1, keepdims=True)
        o_ref[0, sl, :] = x_ref[0, sl, :] + (y * lax.rsqrt(ms + RMS_EPS)) * g_ref[...]


def _out_proj(atts, w, x, g, *, tm):
    B, S, D = x.shape
    n_in = len(atts)
    a_specs = [pl.BlockSpec((1, a.shape[1], tm, LANES), lambda b, s: (b, 0, s, 0)) for a in atts]
    return pl.pallas_call(
        functools.partial(_out_kernel, n_in=n_in),
        grid=(B, S // tm),
        in_specs=a_specs + [
            pl.BlockSpec(w.shape, lambda b, s: (0, 0)),
            pl.BlockSpec((1, tm, D), lambda b, s: (b, s, 0)),
            pl.BlockSpec((1, D), lambda b, s: (0, 0)),
        ],
        out_specs=pl.BlockSpec((1, tm, D), lambda b, s: (b, s, 0)),
        out_shape=jax.ShapeDtypeStruct((B, S, D), F32),
        compiler_params=_cparams(2),
        name=f"out_proj{n_in}",
    )(*atts, w, x, g)


def _rope_tiles(positions):
    inv_freq = ROPE_THETA ** (-jnp.arange(0, ROT_DIM, 2, dtype=F32) / ROT_DIM)
    ang = positions.astype(F32)[..., None] * inv_freq
    cos, sin = jnp.cos(ang), jnp.sin(ang)
    half = ROT_DIM // 2
    one = jnp.ones(ang.shape[:-1] + (HEAD_DIM - ROT_DIM,), F32)
    cos_head = jnp.concatenate([cos, cos, one], axis=-1)
    sin_head = jnp.concatenate([-sin, sin, 0.0 * one], axis=-1)
    assert cos_head.shape[-1] == HEAD_DIM and half * 2 == ROT_DIM
    return jnp.tile(cos_head, (1, 1, 2)), jnp.tile(sin_head, (1, 1, 2))


def _layer0_weights(w_in, b_forget, idx_k_g, idx_k_b):
    scale = HEAD_DIM ** -0.5
    off = 0
    cols = {}
    for name, n in (("fq", WIDTH), ("fk", WIDTH), ("fv", WIDTH), ("fl", FOX_HEADS), ("fg", WIDTH),
                    ("dq", WIDTH), ("dk", WIDTH), ("dv", WIDTH), ("iq", WIDTH), ("ik", IDX_DIM),
                    ("iw", IDX_HEADS), ("dg", WIDTH)):
        cols[name] = w_in[:, off:off + n]
        off += n
    assert off == w_in.shape[1]
    order = [None] * 9
    order[T_DQ], order[T_DK], order[T_IQ] = cols["dq"] * scale, cols["dk"], cols["iq"]
    order[T_FQ], order[T_FK], order[T_FV], order[T_FG] = cols["fq"] * scale, cols["fk"], cols["fv"], cols["fg"]
    order[T_DV], order[T_DG] = cols["dv"], cols["dg"]
    w_main = jnp.concatenate(order, axis=1).astype(BF16)
    pad = jnp.zeros((w_in.shape[0], LANES - L_IW - IDX_HEADS), w_in.dtype)
    w_small = jnp.concatenate([cols["ik"], cols["fl"], cols["iw"], pad], axis=1).astype(BF16)
    par = jnp.zeros((8, LANES), F32)
    par = par.at[0, L_FL:L_FL + FOX_HEADS].set(b_forget.astype(F32))
    par = par.at[1, :IDX_DIM].set(idx_k_g.astype(F32))
    par = par.at[2, :IDX_DIM].set(idx_k_b.astype(F32))
    return w_main, w_small, par


def _even_layer(x, cos_t, sin_t, pre_g, post_g, w_in, b_forget, idx_k_g, idx_k_b, w_out, top_k, cfg):
    w_main, w_small, par = _layer0_weights(w_in, b_forget, idx_k_g, idx_k_b)
    p0, ik2, small = _proj0(x, pre_g[None].astype(F32), w_main, w_small, cos_t, sin_t, par, tm=cfg["tm_proj"])
    fox = _fox(p0, small, tq=cfg["tq"])
    dsa = _dsa(p0, ik2, small, tq=cfg["tq"], top_k=top_k)
    return _out_proj([fox, dsa], w_out.astype(BF16), x, post_g[None].astype(F32), tm=cfg["tm_out"])


def _odd_layer(x, pre_g, post_g, w_in, rel_table, w_out, cfg):
    scale = HEAD_DIM ** -0.5
    width = CHK_HEADS * HEAD_DIM
    w = jnp.concatenate([w_in[:, :width] * scale, w_in[:, width:]], axis=1).astype(BF16)
    p1 = _proj1(x, pre_g[None].astype(F32), w, tm=cfg["tm_proj"], n_tensors=4)
    att = _chunk_attn(p1, _band_bias_t(rel_table, tq=cfg["tq"]), tq=cfg["tq"], n_tensors=4)
    return _out_proj([att], w_out.astype(BF16), x, post_g[None].astype(F32), tm=cfg["tm_out"])


def _config(seq):
    tq = 256
    assert seq % tq == 0 and tq % CHUNK == 0
    tm_proj = min(1024, seq)
    tm_out = min(1024, seq)
    assert seq % tm_proj == 0 and seq % tm_out == 0
    return {"tq": tq, "tm_proj": tm_proj, "tm_out": tm_out}


def kernel(x, positions, pre_norm_g, post_norm_g, w_in_even, b_forget, idx_k_g, idx_k_b,
           w_out_even, w_in_odd, rel_bias, w_out_odd):
    seq = x.shape[1]
    depth = pre_norm_g.shape[0]
    cfg = _config(seq)
    top_k = min(DSA_TOPK, seq // 4)
    cos_t, sin_t = _rope_tiles(positions)
    for layer in range(depth):
        i = layer // 2
        if layer % 2 == 0:
            x = _even_layer(x, cos_t, sin_t, pre_norm_g[layer], post_norm_g[layer], w_in_even[i],
                            b_forget[i], idx_k_g[i], idx_k_b[i], w_out_even[i], top_k, cfg)
        else:
            x = _odd_layer(x, pre_norm_g[layer], post_norm_g[layer], w_in_odd[i], rel_bias[i], w_out_odd[i], cfg)
    return x
```
